```python
import jax, jax.numpy as jnp
from jax import lax
import numpy as np

D_MODEL = 2048
BATCH = 32
SEQ = 256
DEPTH = 1
DEC_BATCH = 8
DEC_SEQ = 2048
PAST_LEN = 256

GRID_W = 64
D_A = 1024
HA_HEADS = 8
HA_DK = 128
HA_DV = 128
CHUNK = 32
D_B = 1024
CONV_K = 31
N_EXPERTS = 32
TOP_K = 4
D_FF = 2048
MOE_BLOCK = 128
SWIGLU_LIMIT = 7.0
SWIGLU_ALPHA = 1.702
EPS = 1e-6
N_IN = 5 * D_A + 2 * D_B + 2 * D_MODEL

kernel_name = "hybrid_hgrn2_conformer_moe_flow_step"


def rms_norm(x, g):
    xf = x.astype(jnp.float32)
    y = xf * lax.rsqrt(jnp.mean(xf * xf, axis=-1, keepdims=True) + EPS)
    return (y * g.astype(jnp.float32)).astype(x.dtype)


def layer_norm(x, g, b):
    xf = x.astype(jnp.float32)
    mu = jnp.mean(xf, axis=-1, keepdims=True)
    var = jnp.mean(jnp.square(xf - mu), axis=-1, keepdims=True)
    y = (xf - mu) * lax.rsqrt(var + EPS)
    return (y * g.astype(jnp.float32) + b.astype(jnp.float32)).astype(x.dtype)


def gla_chunk_scan(q, k, v, log_f, s0):
    B, L, H, DK = q.shape
    DV = v.shape[-1]
    n = L // CHUNK

    def to_chunks(t):
        return t.reshape(B, n, CHUNK, H, t.shape[-1]).transpose(1, 0, 3, 2, 4)

    tril = jnp.tril(jnp.ones((CHUNK, CHUNK), dtype=bool))

    def step(S, inp):
        qc, kc, vc, gc = inp
        b = jnp.cumsum(gc, axis=2)
        o_inter = jnp.einsum('bhtk,bhkv->bhtv', qc * jnp.exp(b), S)
        diff = b[:, :, :, None, :] - b[:, :, None, :, :]
        decay = jnp.exp(jnp.where(tril[:, :, None], diff, -jnp.inf))
        scores = jnp.einsum('bhtk,bhtsk,bhsk->bhts', qc, decay, kc)
        o_intra = jnp.einsum('bhts,bhsv->bhtv', scores, vc)
        b_end = b[:, :, -1:, :]
        S_new = jnp.exp(b_end[:, :, 0, :])[..., None] * S + jnp.einsum(
            'bhsk,bhsv->bhkv', kc * jnp.exp(b_end - b), vc)
        return S_new, o_inter + o_intra

    S_fin, o = lax.scan(step, s0, (to_chunks(q), to_chunks(k), to_chunks(v), to_chunks(log_f)))
    o = o.transpose(1, 0, 3, 2, 4).reshape(B, L, H, DV)
    return o, S_fin


def token_mixer(h, s0, lb, lw, latent):
    B, L, _ = h.shape
    u = h @ lw['w_in']
    q, i, f_fw, f_bw, g_o, glu, gates = jnp.split(
        u, [D_A, 2 * D_A, 3 * D_A, 4 * D_A, 5 * D_A, 5 * D_A + 2 * D_B], axis=-1)

    def heads(t):
        return t.reshape(B, L, HA_HEADS, -1).astype(jnp.float32)

    qh = jax.nn.silu(heads(q))
    vh = heads(i)

    def forget(fr, lbd):
        f = lbd.reshape(HA_HEADS, HA_DK) + (1.0 - lbd.reshape(HA_HEADS, HA_DK)) * jax.nn.sigmoid(fr)
        return 1.0 - f, jnp.log(f)

    k_fw, lf_fw = forget(heads(f_fw), lb[0])
    k_bw, lf_bw = forget(heads(f_bw), lb[1])
    s0f = s0.astype(jnp.float32)
    o_fw, s_fw = gla_chunk_scan(qh, k_fw, vh, lf_fw, s0f[:, 0])
    flip = lambda t: jnp.flip(t, axis=1)
    o_bw, s_bw = gla_chunk_scan(flip(qh), flip(k_bw), flip(vh), flip(lf_bw), s0f[:, 1])
    o = (o_fw + flip(o_bw)).reshape(B, L, D_A)
    o = (rms_norm(o, lw['hgrn_norm_g']) * jax.nn.silu(g_o.astype(jnp.float32))).astype(h.dtype)
    y_a = o @ lw['w_a_out']

    a, bg = jnp.split(glu, 2, axis=-1)
    v = a * jax.nn.sigmoid(bg)
    if latent:
        rows = L // GRID_W
        v = v.reshape(B * rows, GRID_W, D_B)
    v = lax.conv_general_dilated(
        v, lw['conv_w'][:, None, :].astype(v.dtype), window_strides=(1,),
        padding=[(CONV_K // 2, CONV_K // 2)], dimension_numbers=('NWC', 'WIO', 'NWC'),
        feature_group_count=D_B) + lw['conv_b']
    v = v.reshape(B, L, D_B)
    v = jax.nn.silu(layer_norm(v, lw['conv_ln_g'], lw['conv_ln_b']))
    y_b = v @ lw['w_b_out']

    g_a, g_b = jnp.split(gates, 2, axis=-1)
    y = (jax.nn.sigmoid(g_a) * y_a + jax.nn.sigmoid(g_b) * y_b) @ lw['w_out']
    return y, jnp.stack([s_fw, s_bw], axis=1).astype(h.dtype)


def moe(h2, w_router, b_router, w_gate_up, b_gate_up, w_down, b_down):
    T, D = h2.shape
    logits = (h2 @ w_router).astype(jnp.float32) + b_router.astype(jnp.float32)
    top_val, top_idx = lax.top_k(logits, TOP_K)
    probs = jax.nn.softmax(top_val, axis=-1)
    A = T * TOP_K
    flat_e = top_idx.reshape(A)
    flat_tok = jnp.repeat(jnp.arange(T, dtype=jnp.int32), TOP_K)
    flat_p = probs.reshape(A)
    order = jnp.argsort(flat_e)
    e_sorted = flat_e[order]
    counts = jnp.bincount(flat_e, length=N_EXPERTS)
    padded = (counts + MOE_BLOCK - 1) // MOE_BLOCK * MOE_BLOCK
    pad_end = jnp.cumsum(padded)
    pad_start = pad_end - padded
    start = jnp.cumsum(counts) - counts
    dest = pad_start[e_sorted] + (jnp.arange(A, dtype=jnp.int32) - start[e_sorted])
    n_blocks = -(-A // MOE_BLOCK) + N_EXPERTS
    P = n_blocks * MOE_BLOCK
    slot_tok = jnp.full((P,), T, dtype=jnp.int32).at[dest].set(flat_tok[order])
    slot_p = jnp.zeros((P,), jnp.float32).at[dest].set(flat_p[order])
    block_e = jnp.minimum(
        jnp.searchsorted(pad_end, jnp.arange(n_blocks, dtype=jnp.int32) * MOE_BLOCK, side='right'),
        N_EXPERTS - 1)
    h_pad = jnp.concatenate([h2, jnp.zeros((1, D), h2.dtype)], axis=0)

    def expert_block(args):
        tok, e = args
        xb = h_pad[tok]
        gu = xb @ w_gate_up[e] + b_gate_up[e]
        gate, up = jnp.split(gu, 2, axis=-1)
        gate = jnp.minimum(gate, SWIGLU_LIMIT)
        up = jnp.clip(up, -SWIGLU_LIMIT, SWIGLU_LIMIT)
        act = (up + 1.0) * gate * jax.nn.sigmoid(SWIGLU_ALPHA * gate)
        return act @ w_down[e] + b_down[e]

    out = lax.map(expert_block, (slot_tok.reshape(n_blocks, MOE_BLOCK), block_e))
    out = out.reshape(P, D) * slot_p[:, None].astype(out.dtype)
    return jax.ops.segment_sum(out, slot_tok, num_segments=T + 1)[:T]


def trunk_layer(x, cond, s0, latent, lb, lw):
    B, L, D = x.shape
    m = jax.nn.silu(cond) @ lw['w_ada'] + lw['b_ada']
    sh1, sc1, g1, sh2, sc2, g2 = jnp.split(m[:, None, :], 6, axis=-1)
    h = rms_norm(x, lw['norm_mix_g']) * (1.0 + sc1) + sh1
    y, s_fin = token_mixer(h, s0, lb, lw, latent)
    x = x + g1 * y
    h = rms_norm(x, lw['norm_moe_g']) * (1.0 + sc2) + sh2
    y = moe(h.reshape(B * L, D), lw['w_router'], lw['b_router'], lw['w_gate_up'],
            lw['b_gate_up'], lw['w_down'], lw['b_down']).reshape(B, L, D)
    x = x + g2 * y
    return x, s_fin


def setup_inputs(seed: int = 0) -> dict:
    key = jax.random.key(seed)
    ks = jax.random.split(key, 32)
    f32 = jnp.float32
    D = D_MODEL

    def nrm(k, shape, scale):
        return jax.random.normal(k, shape, f32) * scale

    return {
        'x_prompt': nrm(ks[0], (BATCH, SEQ, D), 1.0),
        'x_sample': nrm(ks[1], (DEC_BATCH, DEC_SEQ, D), 1.0),
        'state_hgrn': nrm(ks[2], (DEC_BATCH, DEPTH, 2, HA_HEADS, HA_DK, HA_DV), 0.5),
        'c': nrm(ks[3], (DEC_BATCH, D), 1.0),
        'c_ctx': nrm(ks[4], (D,), 1.0),
        'w_ada': nrm(ks[5], (DEPTH, D, 6 * D), 0.5 * D ** -0.5),
        'b_ada': nrm(ks[6], (DEPTH, 6 * D), 0.02),
        'norm_mix_g': 1.0 + nrm(ks[7], (DEPTH, D), 0.05),
        'w_in': nrm(ks[8], (DEPTH, D, N_IN), D ** -0.5),
        'lb_logits': nrm(ks[9], (DEPTH + 1, 2, D_A), 0.5),
        'hgrn_norm_g': 1.0 + nrm(ks[10], (DEPTH, D_A), 0.05),
        'w_a_out': nrm(ks[11], (DEPTH, D_A, D), D_A ** -0.5),
        'conv_w': nrm(ks[12], (DEPTH, CONV_K, D_B), CONV_K ** -0.5),
        'conv_b': nrm(ks[13], (DEPTH, D_B), 0.02),
        'conv_ln_g': 1.0 + nrm(ks[14], (DEPTH, D_B), 0.05),
        'conv_ln_b': nrm(ks[15], (DEPTH, D_B), 0.02),
        'w_b_out': nrm(ks[16], (DEPTH, D_B, D), D_B ** -0.5),
        'w_out': nrm(ks[17], (DEPTH, D, D), D ** -0.5),
        'norm_moe_g': 1.0 + nrm(ks[18], (DEPTH, D), 0.05),
        'w_router': nrm(ks[19], (DEPTH, D, N_EXPERTS), D ** -0.5),
        'b_router': nrm(ks[20], (DEPTH, N_EXPERTS), 0.01),
        'w_gate_up': nrm(ks[21], (DEPTH, N_EXPERTS, D, 2 * D_FF), D ** -0.5),
        'b_gate_up': nrm(ks[22], (DEPTH, N_EXPERTS, 2 * D_FF), 0.02),
        'w_down': nrm(ks[23], (DEPTH, N_EXPERTS, D_FF, D), D_FF ** -0.5),
        'b_down': nrm(ks[24], (DEPTH, N_EXPERTS, D), 0.02),
        'final_norm_g': 1.0 + nrm(ks[25], (D,), 0.05),
    }


def reference(x_prompt, x_sample, state_hgrn, c, c_ctx, w_ada, b_ada, norm_mix_g, w_in,
              lb_logits, hgrn_norm_g, w_a_out, conv_w, conv_b, conv_ln_g, conv_ln_b, w_b_out,
              w_out, norm_moe_g, w_router, b_router, w_gate_up, b_gate_up, w_down, b_down,
              final_norm_g):
    lb_all = jnp.cumsum(jax.nn.softmax(lb_logits.astype(jnp.float32), axis=0), axis=0)
    n_ctx = x_prompt.shape[0]
    cond_ctx = jnp.broadcast_to(c_ctx, (n_ctx, D_MODEL))
    ctx_init = jnp.zeros((n_ctx, 2, HA_HEADS, HA_DK, HA_DV), x_prompt.dtype)
    xp, xs = x_prompt, x_sample
    ctx_states = []
    for l in range(DEPTH):
        lw = {
            'w_ada': w_ada[l], 'b_ada': b_ada[l], 'norm_mix_g': norm_mix_g[l], 'w_in': w_in[l],
            'hgrn_norm_g': hgrn_norm_g[l], 'w_a_out': w_a_out[l], 'conv_w': conv_w[l],
            'conv_b': conv_b[l], 'conv_ln_g': conv_ln_g[l], 'conv_ln_b': conv_ln_b[l],
            'w_b_out': w_b_out[l], 'w_out': w_out[l], 'norm_moe_g': norm_moe_g[l],
            'w_router': w_router[l], 'b_router': b_router[l], 'w_gate_up': w_gate_up[l],
            'b_gate_up': b_gate_up[l], 'w_down': w_down[l], 'b_down': b_down[l],
        }
        xp, s_ctx = trunk_layer(xp, cond_ctx, ctx_init, False, lb_all[l], lw)
        ctx_states.append(s_ctx)
        xs, _ = trunk_layer(xs, c, state_hgrn[:, l], True, lb_all[l], lw)
    new_state_hgrn = jnp.stack(ctx_states, axis=1)
    y_prompt = rms_norm(xp, final_norm_g)
    y_sample = rms_norm(xs, final_norm_g)
    return (y_prompt, y_sample, new_state_hgrn)
```

```python
import functools

import jax
import jax.numpy as jnp
from jax import lax
from jax.experimental import pallas as pl
from jax.experimental.pallas import tpu as pltpu

F32 = jnp.float32
BF16 = jnp.bfloat16
U32 = jnp.uint32
I32 = jnp.int32

EPS = 1e-6
TOP_K = 4
GRID_W = 64
SWIGLU_LIMIT = 7.0
SWIGLU_ALPHA = 1.702

LANES = 128
SUBLANES = 8
SCAN_CHUNK = 16
CONV_PAD = 16
V7X_VMEM_LIMIT = 56 * 2**20


def _params(*sem):
    return pltpu.CompilerParams(dimension_semantics=sem, vmem_limit_bytes=V7X_VMEM_LIMIT)


def _tile(n, target):
    t = min(n, target) // LANES * LANES
    while n % t:
        t -= LANES
    return t


def _silu(x):
    return x * jax.nn.sigmoid(x)


def _mod_rmsnorm(x, g, scale, shift):
    y = x * lax.rsqrt(jnp.mean(x * x, axis=-1, keepdims=True) + EPS) * g
    return y * (1.0 + scale) + shift


def _ada_kernel(c_ref, w_ref, b_ref, o_ref):
    s = _silu(c_ref[...]).astype(BF16)
    o_ref[...] = jnp.dot(s, w_ref[...].astype(BF16), preferred_element_type=F32) + b_ref[...]


def _adaln(cond, w, b):
    rows, d = cond.shape
    n = w.shape[1]
    tn = _tile(n, 1024)
    return pl.pallas_call(
        _ada_kernel,
        grid=(n // tn,),
        in_specs=[
            pl.BlockSpec((rows, d), lambda j: (0, 0)),
            pl.BlockSpec((d, tn), lambda j: (0, j)),
            pl.BlockSpec((1, tn), lambda j: (0, j)),
        ],
        out_specs=pl.BlockSpec((rows, tn), lambda j: (0, j)),
        out_shape=jax.ShapeDtypeStruct((rows, n), F32),
        compiler_params=_params("arbitrary"),
        name="adaln",
    )(cond, w, b)


def _inproj_kernel(xp_ref, xs_ref, g_ref, sh_ref, sc_ref, w_ref, o_ref, h_scr, *, n_ctx_blocks, head_major):
    m = pl.program_id(0)

    @pl.when(pl.program_id(1) == 0)
    def _():
        x = jnp.where(m < n_ctx_blocks, xp_ref[...], xs_ref[...])
        h_scr[...] = _mod_rmsnorm(x, g_ref[...], sc_ref[...], sh_ref[...]).astype(BF16)

    acc = jnp.dot(h_scr[...], w_ref[...], preferred_element_type=F32)
    if head_major:
        for j in range(o_ref.shape[0]):
            o_ref[j] = acc[:, j * LANES:(j + 1) * LANES].astype(o_ref.dtype)
    else:
        o_ref[...] = acc.astype(o_ref.dtype)


def _inproj(xp, xs, g, mods, w, lay, *, head_major):
    d = xp.shape[1]
    n = w.shape[1]
    bm, bn = lay["bm1"], _tile(n, lay["bn1"])
    ncb = lay["tc"] // bm
    nblk = lay["t"] // bm

    def seq_row(m):
        return jnp.where(m < ncb, lay["ctx_row"], ((m - ncb) * bm) // lay["lat_len"])

    if head_major:
        tiles = bn // LANES
        out_shape = jax.ShapeDtypeStruct((n // LANES, lay["t"], LANES), BF16)
        out_spec = pl.BlockSpec((tiles, bm, LANES), lambda m, j: (j, m, 0))
    else:
        out_shape = jax.ShapeDtypeStruct((lay["t"], n), BF16)
        out_spec = pl.BlockSpec((bm, bn), lambda m, j: (m, j))
    return pl.pallas_call(
        functools.partial(_inproj_kernel, n_ctx_blocks=ncb, head_major=head_major),
        grid=(nblk, n // bn),
        in_specs=[
            pl.BlockSpec((bm, d), lambda m, j: (jnp.minimum(m, ncb - 1), 0)),
            pl.BlockSpec((bm, d), lambda m, j: (jnp.maximum(m - ncb, 0), 0)),
            pl.BlockSpec((1, d), lambda m, j: (0, 0)),
            pl.BlockSpec((None, 1, d), lambda m, j: (seq_row(m) * 6 + 0, 0, 0)),
            pl.BlockSpec((None, 1, d), lambda m, j: (seq_row(m) * 6 + 1, 0, 0)),
            pl.BlockSpec((d, bn), lambda m, j: (0, j)),
        ],
        out_specs=out_spec,
        out_shape=out_shape,
        scratch_shapes=[pltpu.VMEM((bm, d), BF16)],
        compiler_params=_params("parallel", "arbitrary"),
        name="inproj_heads" if head_major else "inproj_rows",
    )(xp, xs, g, mods, mods, w)


def _hgrn_kernel(qf_ref, vf_ref, ff_ref, qb_ref, vb_ref, fb_ref, lb_ref, s0_ref, of_ref, ob_ref, sfin_ref,
                 st, qs_s, qt_s, k_s, kt_s, b_s, v_s, de_s, *, n_ctx_steps, lat_steps_per_seq):
    c = SCAN_CHUNK
    lc = qf_ref.shape[0]
    n_chunks = lc // c
    s = pl.program_id(1)
    is_ctx = s < n_ctx_steps
    j = jnp.where(is_ctx, 0, (s - n_ctx_steps) % lat_steps_per_seq)
    first = jnp.logical_or(is_ctx, j == 0)

    @pl.when(first)
    def _():
        for d in range(2):
            st[d] = jnp.where(is_ctx, 0.0, s0_ref[d].T)

    row = lax.broadcasted_iota(I32, (lc, lc), 0)
    col = lax.broadcasted_iota(I32, (lc, lc), 1)
    same = (row // c) == (col // c)
    blk = same.astype(BF16)

    def split3_dot(a, g):
        g1 = g.astype(BF16)
        r1 = g - g1.astype(F32)
        g2 = r1.astype(BF16)
        g3 = (r1 - g2.astype(F32)).astype(BF16)
        dot = functools.partial(jnp.dot, preferred_element_type=F32)
        return dot(a, g1) + dot(a, g2) + dot(a, g3)

    for d, (q_ref, v_ref, f_ref) in enumerate(((qf_ref, vf_ref, ff_ref), (qb_ref, vb_ref, fb_ref))):
        tri = jnp.logical_and(same, (col <= row) if d == 0 else (col >= row)).astype(BF16)
        qs = _silu(q_ref[...].astype(F32))
        lb = lb_ref[d]
        f = lb + (1.0 - lb) * jax.nn.sigmoid(f_ref[...].astype(F32))
        g = jnp.log(f)
        b = split3_dot(tri, g)
        b_end = split3_dot(blk, g)
        qs_s[d] = qs
        qt_s[d] = qs * jnp.exp(b)
        k_s[d] = 1.0 - f
        kt_s[d] = (1.0 - f) * jnp.exp(b_end - b)
        b_s[d] = b
        v_s[d] = v_ref[...].astype(F32)
        de_s[d] = jnp.exp(b_end)

    ones = jnp.ones((LANES, LANES), BF16)
    t_in_group = lax.broadcasted_iota(I32, (SUBLANES, LANES), 0)
    groups = c // SUBLANES

    def chunk_step(ci, carry):
        for d, o_ref in enumerate((of_ref, ob_ref)):
            cidx = ci if d == 0 else n_chunks - 1 - ci
            r0 = pl.multiple_of(cidx * c, c)
            rows = pl.ds(r0, c)
            qs, qt, kk, kt, b, v = (ref[d, rows, :] for ref in (qs_s, qt_s, k_s, kt_s, b_s, v_s))
            state = st[d]
            o = lax.dot_general(qt.astype(BF16), state.astype(BF16), (((1,), (1,)), ((), ())),
                                preferred_element_type=F32)
            terms = []
            for si in range(c):
                gs = si // SUBLANES
                for gi in (range(gs, groups) if d == 0 else range(0, gs + 1)):
                    sl = slice(gi * SUBLANES, (gi + 1) * SUBLANES)
                    diff = b[sl] - b[si:si + 1]
                    if gi == gs:
                        t = t_in_group + gi * SUBLANES
                        diff = jnp.where((t >= si) if d == 0 else (t <= si), diff, -jnp.inf)
                    terms.append((si, gi, qs[sl] * kk[si:si + 1] * jnp.exp(diff)))
            stacked = jnp.concatenate([x for _, _, x in terms], axis=0).astype(BF16)
            score = jnp.dot(stacked, ones, preferred_element_type=F32)
            parts = [o[gi * SUBLANES:(gi + 1) * SUBLANES] for gi in range(groups)]
            for i, (si, gi, _) in enumerate(terms):
                parts[gi] = parts[gi] + score[i * SUBLANES:(i + 1) * SUBLANES] * v[si:si + 1]
            o_ref[rows, :] = jnp.concatenate(parts, axis=0).astype(o_ref.dtype)
            upd = lax.dot_general(v.astype(BF16), kt.astype(BF16), (((0,), (0,)), ((), ())),
                                  preferred_element_type=F32)
            st[d] = state * de_s[d, pl.ds(r0, 1), :] + upd
        return carry

    lax.fori_loop(0, n_chunks, chunk_step, 0)

    @pl.when(is_ctx)
    def _():
        for d in range(2):
            sfin_ref[d] = st[d].T


def _hgrn(uh, lb, s0, lay):
    h = lay["heads"]
    lc = lay["lc"]
    ncs = lay["tc"] // lc
    spl = lay["lat_len"] // lc
    nls = lay["tl"] // lc
    assert lay["ctx_len"] == lc

    def fwd_blk(s):
        return s

    def bwd_blk(s):
        q = s - ncs
        return jnp.where(s < ncs, s, ncs + (q // spl) * spl + (spl - 1 - q % spl))

    def lat_seq(s):
        return jnp.maximum(s - ncs, 0) // spl

    def part(p, blk):
        return pl.BlockSpec((None, lc, LANES), lambda hh, s: (p * h + hh, blk(s), 0))

    o_spec_f = pl.BlockSpec((None, lc, LANES), lambda hh, s: (hh, fwd_blk(s), 0))
    o_spec_b = pl.BlockSpec((None, lc, LANES), lambda hh, s: (hh, bwd_blk(s), 0))
    state_shape = (lay["ctx_batch"], 2, h, LANES, LANES)
    sfin_spec = pl.BlockSpec((None, 2, None, LANES, LANES), lambda hh, s: (jnp.minimum(s, ncs - 1), 0, hh, 0, 0))
    scr = pltpu.VMEM((2, lc, LANES), F32)
    return pl.pallas_call(
        functools.partial(_hgrn_kernel, n_ctx_steps=ncs, lat_steps_per_seq=spl),
        grid=(h, ncs + nls),
        in_specs=[
            part(0, fwd_blk), part(1, fwd_blk), part(2, fwd_blk),
            part(0, bwd_blk), part(1, bwd_blk), part(3, bwd_blk),
            pl.BlockSpec((2, None, 1, LANES), lambda hh, s: (0, hh, 0, 0)),
            pl.BlockSpec((None, 2, None, LANES, LANES), lambda hh, s: (lat_seq(s), 0, hh, 0, 0)),
        ],
        out_specs=[o_spec_f, o_spec_b, sfin_spec],
        out_shape=[
            jax.ShapeDtypeStruct((h, lay["t"], LANES), BF16),
            jax.ShapeDtypeStruct((h, lay["t"], LANES), BF16),
            jax.ShapeDtypeStruct(state_shape, F32),
        ],
        scratch_shapes=[pltpu.VMEM((2, LANES, LANES), F32)] + [scr] * 7,
        compiler_params=_params("arbitrary", "arbitrary"),
        name="hgrn_scan",
    )(uh, uh, uh, uh, uh, uh, lb, s0)


def _branches_kernel(of_ref, ob_ref, ur_ref, hg_ref, cw_ref, cb_ref, lng_ref, lnb_ref, wa_ref, wb_ref, o_ref,
                     pad_c, pad_l, cv, *, n_ctx_blocks, seg_ctx, seg_lat, heads, d_a, d_b, d_model):
    bm = o_ref.shape[0]
    m = pl.program_id(0)
    taps = cw_ref.shape[0]
    half = taps // 2

    o_h = [of_ref[h].astype(F32) + ob_ref[h].astype(F32) for h in range(heads)]
    ss = o_h[0] * o_h[0]
    for h in range(1, heads):
        ss = ss + o_h[h] * o_h[h]
    inv = lax.rsqrt(jnp.sum(ss, axis=-1, keepdims=True) / d_a + EPS)
    gated = []
    for h in range(heads):
        sl = slice(h * LANES, (h + 1) * LANES)
        gated.append((o_h[h] * inv * hg_ref[:, sl]) * _silu(ur_ref[:, sl].astype(F32)))
    y_a = jnp.dot(jnp.concatenate(gated, axis=1).astype(BF16), wa_ref[...], preferred_element_type=F32)

    glu = ur_ref[:, d_a:d_a + d_b].astype(F32) * jax.nn.sigmoid(ur_ref[:, d_a + d_b:d_a + 2 * d_b].astype(F32))

    def conv(pad, seg):
        stride = seg + 2 * CONV_PAD
        n_seg = bm // seg
        zeros = jnp.zeros((CONV_PAD, d_b), F32)
        for i in range(n_seg):
            pad[i * stride:i * stride + CONV_PAD, :] = zeros
            pad[i * stride + CONV_PAD:i * stride + CONV_PAD + seg, :] = glu[i * seg:(i + 1) * seg]
            pad[i * stride + CONV_PAD + seg:(i + 1) * stride, :] = zeros
        rows = min(seg, 64)
        cols = 256
        for i in range(n_seg):
            for r in range(seg // rows):
                base = i * stride + CONV_PAD + r * rows - half

                def col_step(ci, carry, base=base, i=i, r=r):
                    c0 = pl.multiple_of(ci * cols, cols)
                    acc = jnp.zeros((rows, cols), F32)
                    for j in range(taps):
                        acc = acc + pad[base + j:base + j + rows, pl.ds(c0, cols)] * cw_ref[j:j + 1, pl.ds(c0, cols)]
                    cv[i * seg + r * rows:i * seg + (r + 1) * rows, pl.ds(c0, cols)] = acc
                    return carry

                lax.fori_loop(0, d_b // cols, col_step, 0)

    @pl.when(m < n_ctx_blocks)
    def _():
        conv(pad_c, seg_ctx)

    @pl.when(m >= n_ctx_blocks)
    def _():
        conv(pad_l, seg_lat)

    v = cv[...] + cb_ref[...]
    mu = jnp.mean(v, axis=-1, keepdims=True)
    var = jnp.mean(jnp.square(v - mu), axis=-1, keepdims=True)
    v = (v - mu) * lax.rsqrt(var + EPS) * lng_ref[...] + lnb_ref[...]
    y_b = jnp.dot(_silu(v).astype(BF16), wb_ref[...], preferred_element_type=F32)

    g0 = d_a + 2 * d_b
    g_a = jax.nn.sigmoid(ur_ref[:, g0:g0 + d_model].astype(F32))
    g_b = jax.nn.sigmoid(ur_ref[:, g0 + d_model:g0 + 2 * d_model].astype(F32))
    o_ref[...] = (g_a * y_a + g_b * y_b).astype(o_ref.dtype)


def _branches(o_f, o_b, ur, hg, cw, cb, lng, lnb, wa, wb, lay):
    heads, d_a, d_b, d = lay["heads"], lay["d_a"], lay["d_b"], lay["d"]
    bm = lay["bm3"]
    ncb = lay["tc"] // bm
    seg_c = min(lay["ctx_len"], bm)
    seg_l = GRID_W
    nr = ur.shape[1]
    taps = cw.shape[0]
    const = lambda m: (0, 0)
    resident = dict(pipeline_mode=pl.Buffered(1))

    def pad_rows(seg):
        return (bm // seg) * (seg + 2 * CONV_PAD)

    return pl.pallas_call(
        functools.partial(_branches_kernel, n_ctx_blocks=ncb, seg_ctx=seg_c, seg_lat=seg_l, heads=heads, d_a=d_a,
                          d_b=d_b, d_model=d),
        grid=(lay["t"] // bm,),
        in_specs=[
            pl.BlockSpec((heads, bm, LANES), lambda m: (0, m, 0)),
            pl.BlockSpec((heads, bm, LANES), lambda m: (0, m, 0)),
            pl.BlockSpec((bm, nr), lambda m: (m, 0)),
            pl.BlockSpec((1, d_a), const),
            pl.BlockSpec((taps, d_b), const),
            pl.BlockSpec((1, d_b), const),
            pl.BlockSpec((1, d_b), const),
            pl.BlockSpec((1, d_b), const),
            pl.BlockSpec((d_a, d), const, **resident),
            pl.BlockSpec((d_b, d), const, **resident),
        ],
        out_specs=pl.BlockSpec((bm, d), lambda m: (m, 0)),
        out_shape=jax.ShapeDtypeStruct((lay["t"], d), BF16),
        scratch_shapes=[
            pltpu.VMEM((pad_rows(seg_c), d_b), F32),
            pltpu.VMEM((pad_rows(seg_l), d_b), F32),
            pltpu.VMEM((bm, d_b), F32),
        ],
        compiler_params=_params("parallel"),
        name="branches_merge",
    )(o_f, o_b, ur, hg, cw, cb, lng, lnb, wa, wb)


def _pack_bf16_pairs(lo, hi):
    lo_bits = lax.bitcast_convert_type(lo, U32)
    hi_bits = lax.bitcast_convert_type(hi, U32)
    return (lo_bits >> 16) | (hi_bits & jnp.uint32(0xFFFF0000))


def _unpack_bf16_pairs(packed):
    lo = lax.bitcast_convert_type(packed << 16, F32)
    hi = lax.bitcast_convert_type(packed & jnp.uint32(0xFFFF0000), F32)
    return lo, hi


def _outproj_kernel(mix_ref, xp_ref, xs_ref, g1_ref, ng_ref, sh2_ref, sc2_ref, wo_ref, wr_ref, br_ref,
                    x1_ref, hp_ref, rt_ref, *, n_ctx_blocks, n_experts):
    m = pl.program_id(0)
    x = jnp.where(m < n_ctx_blocks, xp_ref[...], xs_ref[...])
    y = jnp.dot(mix_ref[...], wo_ref[...], preferred_element_type=F32)
    x1 = x + g1_ref[...] * y
    x1_ref[...] = x1
    hb = _mod_rmsnorm(x1, ng_ref[...], sc2_ref[...], sh2_ref[...]).astype(BF16)
    hf = hb.astype(F32)
    half = hf.shape[1] // 2
    hp_ref[...] = _pack_bf16_pairs(hf[:, :half], hf[:, half:])

    logits = jnp.dot(hb, wr_ref[...], preferred_element_type=F32) + br_ref[...]
    lane = lax.broadcasted_iota(I32, logits.shape, 1).astype(F32)
    logits = jnp.where(lane < n_experts, logits, -jnp.inf)
    vals, idxs = [], []
    for _ in range(TOP_K):
        mx = jnp.max(logits, axis=-1, keepdims=True)
        ix = jnp.min(jnp.where(logits == mx, lane, float(LANES)), axis=-1, keepdims=True)
        vals.append(mx)
        idxs.append(ix)
        logits = jnp.where(lane == ix, -jnp.inf, logits)
    es = [jnp.exp(v - vals[0]) for v in vals]
    tot = es[0]
    for e in es[1:]:
        tot = tot + e
    out = jnp.zeros(logits.shape, F32)
    for k in range(TOP_K):
        out = jnp.where(lane == k, idxs[k], out)
        out = jnp.where(lane == TOP_K + k, es[k] / tot, out)
    rt_ref[...] = out


def _outproj(mix, xp, xs, mods, ng, wo, wr, br, lay):
    d = lay["d"]
    bm = lay["bm3"]
    ncb = lay["tc"] // bm
    t = lay["t"]

    def seq_row(m):
        return jnp.where(m < ncb, lay["ctx_row"], ((m - ncb) * bm) // lay["lat_len"])

    const = lambda m: (0, 0)
    mod = lambda part: pl.BlockSpec((None, 1, d), lambda m: (seq_row(m) * 6 + part, 0, 0))
    return pl.pallas_call(
        functools.partial(_outproj_kernel, n_ctx_blocks=ncb, n_experts=lay["experts"]),
        grid=(t // bm,),
        in_specs=[
            pl.BlockSpec((bm, d), lambda m: (m, 0)),
            pl.BlockSpec((bm, d), lambda m: (jnp.minimum(m, ncb - 1), 0)),
            pl.BlockSpec((bm, d), lambda m: (jnp.maximum(m - ncb, 0), 0)),
            mod(2),
            pl.BlockSpec((1, d), const),
            mod(3),
            mod(4),
            pl.BlockSpec((d, d), const, pipeline_mode=pl.Buffered(1)),
            pl.BlockSpec((d, LANES), const),
            pl.BlockSpec((1, LANES), const),
        ],
        out_specs=[
            pl.BlockSpec((bm, d), lambda m: (m, 0)),
            pl.BlockSpec((bm, d // 2), lambda m: (m, 0)),
            pl.BlockSpec((bm, LANES), lambda m: (m, 0)),
        ],
        out_shape=[
            jax.ShapeDtypeStruct((t, d), F32),
            jax.ShapeDtypeStruct((t, d // 2), U32),
            jax.ShapeDtypeStruct((t, LANES), F32),
        ],
        compiler_params=_params("parallel"),
        name="outproj_router",
    )(mix, xp, xs, mods, ng, mods, mods, wo, wr, br)


def _routing_tables(idx, prob, lay):
    t, e, bm = lay["t"], lay["experts"], lay["bm_moe"]
    a = t * TOP_K
    nb = lay["moe_blocks"]
    flat_e = idx.reshape(a)
    order = jnp.argsort(flat_e, stable=True).astype(I32)
    e_sorted = flat_e[order]
    counts = jnp.bincount(flat_e, length=e).astype(I32)
    nblk = (counts + bm - 1) // bm
    blk_end = jnp.cumsum(nblk).astype(I32)
    blk_start = blk_end - nblk
    start = jnp.cumsum(counts).astype(I32) - counts
    dest = blk_start[e_sorted] * bm + (jnp.arange(a, dtype=I32) - start[e_sorted])
    slot_tok = jnp.zeros((nb * bm,), I32).at[dest].set(order // TOP_K)
    slot_p = jnp.zeros((nb * bm,), F32).at[dest].set(prob.reshape(a)[order])
    pos = jnp.zeros((a,), I32).at[order].set(dest)
    n_real = blk_end[-1]

    def schedule(nt):
        s = jnp.arange(nb * nt, dtype=I32)
        step_end = blk_end * nt
        ex = jnp.minimum(jnp.searchsorted(step_end, s, side="right").astype(I32), e - 1)
        r = s - blk_start[ex] * nt
        per = jnp.maximum(nblk[ex], 1)
        real = s < n_real * nt
        tail = s - n_real * nt
        m_idx = jnp.where(real, blk_start[ex] + r % per, n_real + tail // nt)
        n_out = jnp.where(real, r // per, tail % nt)
        last = jnp.maximum(n_real * nt - 1, 0)
        e_w = jnp.where(real, ex, ex[last])
        n_w = jnp.where(real, n_out, (r // per)[last])
        flag = jnp.where(real, 1 + 2 * (r % per == 0).astype(I32), 0)
        return m_idx.astype(I32), n_out.astype(I32), e_w.astype(I32), n_w.astype(I32), flag.astype(I32)

    return slot_tok, slot_p, pos, schedule(lay["nt_up"]), schedule(lay["nt_down"])


def _gather_kernel(idx_ref, src_ref, dst_ref, sem):
    i = pl.program_id(0)
    rows = idx_ref.shape[1]

    def issue(r, carry):
        tok = idx_ref[0, r]
        pltpu.make_async_copy(src_ref.at[pl.ds(tok, 1), :], dst_ref.at[pl.ds(i * rows + r, 1), :], sem).start()
        return carry

    lax.fori_loop(0, rows, issue, 0)
    pltpu.make_async_copy(dst_ref.at[pl.ds(0, rows), :], dst_ref.at[pl.ds(0, rows), :], sem).wait()


def _gather_rows(src, slot_tok, lay):
    bm = lay["bm_moe"]
    nb = lay["moe_blocks"]
    width = src.shape[1]
    return pl.pallas_call(
        _gather_kernel,
        grid=(nb,),
        in_specs=[
            pl.BlockSpec((None, 1, bm), lambda i: (i, 0, 0), memory_space=pltpu.SMEM),
            pl.BlockSpec(memory_space=pl.ANY),
        ],
        out_specs=pl.BlockSpec(memory_space=pl.ANY),
        out_shape=jax.ShapeDtypeStruct((nb * bm, width), src.dtype),
        scratch_shapes=[pltpu.SemaphoreType.DMA(())],
        compiler_params=_params("arbitrary"),
        name="moe_gather",
    )(slot_tok.reshape(nb, 1, bm), src)


def _up_kernel(sm, sn, se, sw, sf, x_ref, wg_ref, wu_ref, bg_ref, bu_ref, o_ref, wg_bf, wu_bf):
    flag = sf[pl.program_id(0)]

    @pl.when(flag >= 2)
    def _():
        wg_bf[...] = wg_ref[...].astype(BF16)
        wu_bf[...] = wu_ref[...].astype(BF16)

    @pl.when(flag >= 1)
    def _():
        lo, hi = _unpack_bf16_pairs(x_ref[...])
        x = jnp.concatenate([lo.astype(BF16), hi.astype(BF16)], axis=1)
        gate = jnp.dot(x, wg_bf[...], preferred_element_type=F32) + bg_ref[...]
        up = jnp.dot(x, wu_bf[...], preferred_element_type=F32) + bu_ref[...]
        gate = jnp.minimum(gate, SWIGLU_LIMIT)
        up = jnp.clip(up, -SWIGLU_LIMIT, SWIGLU_LIMIT)
        o_ref[...] = ((up + 1.0) * gate * jax.nn.sigmoid(SWIGLU_ALPHA * gate)).astype(o_ref.dtype)

    @pl.when(flag == 0)
    def _():
        o_ref[...] = jnp.zeros(o_ref.shape, o_ref.dtype)


def _moe_up(xs, w_gu, b_gu, sched, lay):
    bm, nb, nt = lay["bm_moe"], lay["moe_blocks"], lay["nt_up"]
    d, ff = lay["d"], lay["d_ff"]
    tn = ff // nt
    e = lay["experts"]
    grid_spec = pltpu.PrefetchScalarGridSpec(
        num_scalar_prefetch=5,
        grid=(nb * nt,),
        in_specs=[
            pl.BlockSpec((bm, d // 2), lambda s, sm, sn, se, sw, sf: (sm[s], 0)),
            pl.BlockSpec((None, d, tn), lambda s, sm, sn, se, sw, sf: (se[s], 0, sw[s])),
            pl.BlockSpec((None, d, tn), lambda s, sm, sn, se, sw, sf: (se[s], 0, nt + sw[s])),
            pl.BlockSpec((None, 1, tn), lambda s, sm, sn, se, sw, sf: (se[s], 0, sw[s])),
            pl.BlockSpec((None, 1, tn), lambda s, sm, sn, se, sw, sf: (se[s], 0, nt + sw[s])),
        ],
        out_specs=pl.BlockSpec((bm, tn), lambda s, sm, sn, se, sw, sf: (sm[s], sn[s])),
        scratch_shapes=[pltpu.VMEM((d, tn), BF16), pltpu.VMEM((d, tn), BF16)],
    )
    return pl.pallas_call(
        _up_kernel,
        grid_spec=grid_spec,
        out_shape=jax.ShapeDtypeStruct((nb * bm, ff), BF16),
        compiler_params=_params("arbitrary"),
        name="moe_gate_up",
    )(*sched, xs, w_gu, w_gu, b_gu.reshape(e, 1, 2 * ff), b_gu.reshape(e, 1, 2 * ff))


def _down_kernel(sm, sn, se, sw, sf, a_ref, p_ref, wl_ref, wh_ref, bl_ref, bh_ref, o_ref, wl_bf, wh_bf):
    flag = sf[pl.program_id(0)]

    @pl.when(flag >= 2)
    def _():
        wl_bf[...] = wl_ref[...].astype(BF16)
        wh_bf[...] = wh_ref[...].astype(BF16)

    @pl.when(flag >= 1)
    def _():
        a = a_ref[...]
        p = p_ref[...]
        lo = (jnp.dot(a, wl_bf[...], preferred_element_type=F32) + bl_ref[...]) * p
        hi = (jnp.dot(a, wh_bf[...], preferred_element_type=F32) + bh_ref[...]) * p
        o_ref[...] = _pack_bf16_pairs(lo.astype(BF16).astype(F32), hi.astype(BF16).astype(F32))

    @pl.when(flag == 0)
    def _():
        o_ref[...] = jnp.zeros(o_ref.shape, o_ref.dtype)


def _moe_down(act, slot_p, w_dn, b_dn, sched, lay):
    bm, nb, nt = lay["bm_moe"], lay["moe_blocks"], lay["nt_down"]
    d, ff = lay["d"], lay["d_ff"]
    half = d // 2
    tn = half // nt
    e = lay["experts"]
    grid_spec = pltpu.PrefetchScalarGridSpec(
        num_scalar_prefetch=5,
        grid=(nb * nt,),
        in_specs=[
            pl.BlockSpec((bm, ff), lambda s, sm, sn, se, sw, sf: (sm[s], 0)),
            pl.BlockSpec((bm, 1), lambda s, sm, sn, se, sw, sf: (sm[s], 0)),
            pl.BlockSpec((None, ff, tn), lambda s, sm, sn, se, sw, sf: (se[s], 0, sw[s])),
            pl.BlockSpec((None, ff, tn), lambda s, sm, sn, se, sw, sf: (se[s], 0, nt + sw[s])),
            pl.BlockSpec((None, 1, tn), lambda s, sm, sn, se, sw, sf: (se[s], 0, sw[s])),
            pl.BlockSpec((None, 1, tn), lambda s, sm, sn, se, sw, sf: (se[s], 0, nt + sw[s])),
        ],
        out_specs=pl.BlockSpec((bm, tn), lambda s, sm, sn, se, sw, sf: (sm[s], sn[s])),
        scratch_shapes=[pltpu.VMEM((ff, tn), BF16), pltpu.VMEM((ff, tn), BF16)],
    )
    return pl.pallas_call(
        _down_kernel,
        grid_spec=grid_spec,
        out_shape=jax.ShapeDtypeStruct((nb * bm, half), U32),
        compiler_params=_params("arbitrary"),
        name="moe_down",
    )(*sched, act, slot_p.reshape(nb * bm, 1), w_dn, w_dn, b_dn.reshape(e, 1, d), b_dn.reshape(e, 1, d))


def _combine_kernel(pos_ref, y_ref, x1_ref, g2_ref, fg_ref, o_ref, buf, sem):
    bt = o_ref.shape[0]
    n = pos_ref.shape[1]

    def issue(r, carry):
        pltpu.make_async_copy(y_ref.at[pl.ds(pos_ref[0, r], 1), :], buf.at[pl.ds(r, 1), :], sem).start()
        return carry

    lax.fori_loop(0, n, issue, 0)
    pltpu.make_async_copy(y_ref.at[pl.ds(0, n), :], buf, sem).wait()
    lo, hi = _unpack_bf16_pairs(buf[0:bt, :])
    for k in range(1, TOP_K):
        lo_k, hi_k = _unpack_bf16_pairs(buf[k * bt:(k + 1) * bt, :])
        lo = lo + lo_k
        hi = hi + hi_k
    x2 = x1_ref[...] + g2_ref[...] * jnp.concatenate([lo, hi], axis=1)
    o_ref[...] = x2 * lax.rsqrt(jnp.mean(x2 * x2, axis=-1, keepdims=True) + EPS) * fg_ref[...]


def _combine(y_sorted, pos, x1, mods, fg, lay, *, blk0, n_tok, seq_len, row0):
    d = lay["d"]
    bt = lay["bt"]
    nblk = n_tok // bt

    def seq_row(m):
        return row0 if seq_len is None else row0 + (m * bt) // seq_len

    return pl.pallas_call(
        _combine_kernel,
        grid=(nblk,),
        in_specs=[
            pl.BlockSpec((None, 1, TOP_K * bt), lambda m: (blk0 + m, 0, 0), memory_space=pltpu.SMEM),
            pl.BlockSpec(memory_space=pl.ANY),
            pl.BlockSpec((bt, d), lambda m: (blk0 + m, 0)),
            pl.BlockSpec((None, 1, d), lambda m: (seq_row(m) * 6 + 5, 0, 0)),
            pl.BlockSpec((1, d), lambda m: (0, 0)),
        ],
        out_specs=pl.BlockSpec((bt, d), lambda m: (m, 0)),
        out_shape=jax.ShapeDtypeStruct((n_tok, d), F32),
        scratch_shapes=[pltpu.VMEM((TOP_K * bt, d // 2), U32), pltpu.SemaphoreType.DMA(())],
        compiler_params=_params("arbitrary"),
        name="moe_combine_final",
    )(pos, y_sorted, x1, mods, fg)


def _layout(x_prompt, x_sample, state_hgrn, w_gate_up):
    cb, cl, d = x_prompt.shape
    lb_, ll, _ = x_sample.shape
    heads = state_hgrn.shape[3]
    experts, _, ff2 = w_gate_up.shape[1:]
    tc, tl = cb * cl, lb_ * ll
    t = tc + tl
    bm_moe = 512 if t * TOP_K >= 32768 else 128
    lay = dict(
        d=d, t=t, tc=tc, tl=tl, ctx_batch=cb, ctx_len=cl, lat_batch=lb_, lat_len=ll, ctx_row=lb_,
        heads=heads, d_a=heads * LANES, experts=experts, d_ff=ff2 // 2,
        bm1=512 if (tc % 512 == 0 and ll % 512 == 0) else cl, bn1=min(1024, heads * LANES), lc=cl, bm3=cl,
        bt=min(256, cl),
        bm_moe=bm_moe, moe_blocks=pl.cdiv(t * TOP_K, bm_moe) + experts,
        nt_up=max(1, (ff2 // 2) // 512), nt_down=max(1, (d // 2) // 512),
    )
    assert state_hgrn.shape[4] == LANES and state_hgrn.shape[5] == LANES
    assert tc % lay["bm1"] == 0 and tl % lay["bm1"] == 0 and ll % lay["bm1"] == 0
    assert ll % cl == 0 and cl % GRID_W == 0 and cl % SCAN_CHUNK == 0
    return lay


def kernel(x_prompt, x_sample, state_hgrn, c, c_ctx, w_ada, b_ada, norm_mix_g, w_in, lb_logits, hgrn_norm_g, w_a_out, conv_w, conv_b, conv_ln_g, conv_ln_b, w_b_out, w_out, norm_moe_g, w_router, b_router, w_gate_up, b_gate_up, w_down, b_down, final_norm_g):
    assert w_ada.shape[0] == 1, "one trunk layer"
    lay = _layout(x_prompt, x_sample, state_hgrn, w_gate_up)
    d, d_a, heads = lay["d"], lay["d_a"], lay["heads"]
    d_b = conv_w.shape[2]
    lay["d_b"] = d_b
    xp = x_prompt.reshape(lay["tc"], d)
    xs = x_sample.reshape(lay["tl"], d)

    n_rows = lay["lat_batch"] + 1
    rows_pad = pl.cdiv(n_rows, SUBLANES) * SUBLANES
    cond = jnp.concatenate([c, c_ctx[None, :], jnp.zeros((rows_pad - n_rows, d), F32)], axis=0)
    mods = _adaln(cond, w_ada.reshape(d, 6 * d), b_ada.reshape(1, 6 * d)).reshape(rows_pad * 6, 1, d)

    w_in_bf = w_in.reshape(d, -1).astype(BF16)
    n_scan = 4 * d_a
    g_mix = norm_mix_g.reshape(1, d)
    uh = _inproj(xp, xs, g_mix, mods, w_in_bf[:, :n_scan], lay, head_major=True)
    ur = _inproj(xp, xs, g_mix, mods, w_in_bf[:, n_scan:], lay, head_major=False)

    lb_all = jnp.cumsum(jax.nn.softmax(lb_logits.astype(F32), axis=0), axis=0)[0]
    s0 = state_hgrn.reshape(lay["lat_batch"], 2, heads, LANES, LANES)
    o_f, o_b, s_fin = _hgrn(uh, lb_all.reshape(2, heads, 1, LANES), s0, lay)

    mix = _branches(o_f, o_b, ur, hgrn_norm_g.reshape(1, d_a), conv_w.reshape(-1, d_b), conv_b.reshape(1, d_b),
                    conv_ln_g.reshape(1, d_b), conv_ln_b.reshape(1, d_b), w_a_out.reshape(d_a, d).astype(BF16),
                    w_b_out.reshape(d_b, d).astype(BF16), lay)

    e, ff = lay["experts"], lay["d_ff"]
    wr = jnp.zeros((d, LANES), BF16).at[:, :e].set(w_router.reshape(d, e).astype(BF16))
    br = jnp.zeros((1, LANES), F32).at[:, :e].set(b_router.reshape(1, e))
    x1, hp, rt = _outproj(mix, xp, xs, mods, norm_moe_g.reshape(1, d), w_out.reshape(d, d).astype(BF16), wr, br,
                          lay)

    idx = rt[:, :TOP_K].astype(I32)
    prob = rt[:, TOP_K:2 * TOP_K]
    slot_tok, slot_p, pos, sched_up, sched_down = _routing_tables(idx, prob, lay)
    x_sorted = _gather_rows(hp, slot_tok, lay)
    act = _moe_up(x_sorted, w_gate_up.reshape(e, d, 2 * ff), b_gate_up.reshape(e, 2 * ff), sched_up, lay)
    y_sorted = _moe_down(act, slot_p, w_down.reshape(e, ff, d), b_down.reshape(e, d), sched_down, lay)

    bt = lay["bt"]
    pos_blocks = pos.reshape(lay["t"] // bt, bt, TOP_K).transpose(0, 2, 1).reshape(lay["t"] // bt, 1, TOP_K * bt)
    fg = final_norm_g.reshape(1, d)
    y_prompt = _combine(y_sorted, pos_blocks, x1, mods, fg, lay, blk0=0, n_tok=lay["tc"],
                        seq_len=None, row0=lay["ctx_row"])
    y_sample = _combine(y_sorted, pos_blocks, x1, mods, fg, lay, blk0=lay["tc"] // bt, n_tok=lay["tl"],
                        seq_len=lay["lat_len"], row0=0)
    return (y_prompt.reshape(x_prompt.shape), y_sample.reshape(x_sample.shape),
            s_fin.reshape(lay["ctx_batch"], 1, 2, heads, LANES, LANES))
```

```python
import functools

import jax
import jax.numpy as jnp
from jax import lax
from jax.experimental import pallas as pl
from jax.experimental.pallas import tpu as pltpu

F32 = jnp.float32
BF16 = jnp.bfloat16
U32 = jnp.uint32
I32 = jnp.int32

EPS = 1e-6
TOP_K = 4
GRID_W = 64
SWIGLU_LIMIT = 7.0
SWIGLU_ALPHA = 1.702

LANES = 128
SUBLANES = 8
SCAN_CHUNK = 16
CONV_PAD = 16
V7X_VMEM_LIMIT = 56 * 2**20


def _params(*sem):
    return pltpu.CompilerParams(dimension_semantics=sem, vmem_limit_bytes=V7X_VMEM_LIMIT)


def _tile(n, target):
    t = min(n, target) // LANES * LANES
    while n % t:
        t -= LANES
    return t


def _silu(x):
    return x * jax.nn.sigmoid(x)


def _mod_rmsnorm(x, g, scale, shift):
    y = x * lax.rsqrt(jnp.mean(x * x, axis=-1, keepdims=True) + EPS) * g
    return y * (1.0 + scale) + shift


def _ada_kernel(c_ref, w_ref, b_ref, o_ref):
    s = _silu(c_ref[...]).astype(BF16)
    o_ref[...] = jnp.dot(s, w_ref[...].astype(BF16), preferred_element_type=F32) + b_ref[...]


def _adaln(cond, w, b):
    rows, d = cond.shape
    n = w.shape[1]
    tn = _tile(n, 1024)
    return pl.pallas_call(
        _ada_kernel,
        grid=(n // tn,),
        in_specs=[
            pl.BlockSpec((rows, d), lambda j: (0, 0)),
            pl.BlockSpec((d, tn), lambda j: (0, j)),
            pl.BlockSpec((1, tn), lambda j: (0, j)),
        ],
        out_specs=pl.BlockSpec((rows, tn), lambda j: (0, j)),
        out_shape=jax.ShapeDtypeStruct((rows, n), F32),
        compiler_params=_params("arbitrary"),
        name="adaln",
    )(cond, w, b)


def _inproj_kernel(xp_ref, xs_ref, g_ref, sh_ref, sc_ref, w_ref, o_ref, h_scr, *, n_ctx_blocks):
    m = pl.program_id(0)

    @pl.when(pl.program_id(1) == 0)
    def _():
        x = jnp.where(m < n_ctx_blocks, xp_ref[...], xs_ref[...])
        h_scr[...] = _mod_rmsnorm(x, g_ref[...], sc_ref[...], sh_ref[...]).astype(BF16)

    acc = jnp.dot(h_scr[...], w_ref[...], preferred_element_type=F32)
    for j in range(o_ref.shape[0]):
        o_ref[j] = acc[:, j * LANES:(j + 1) * LANES].astype(o_ref.dtype)


def _inproj(xp, xs, g, mods, w, lay):
    d = xp.shape[1]
    n = w.shape[1]
    bm, bn = lay["bm1"], _tile(n, lay["bn1"])
    ncb = lay["tc"] // bm
    nblk = lay["t"] // bm

    def seq_row(m):
        return jnp.where(m < ncb, lay["ctx_row"], ((m - ncb) * bm) // lay["lat_len"])

    out_shape = jax.ShapeDtypeStruct((n // LANES, lay["t"], LANES), BF16)
    out_spec = pl.BlockSpec((bn // LANES, bm, LANES), lambda m, j: (j, m, 0))
    return pl.pallas_call(
        functools.partial(_inproj_kernel, n_ctx_blocks=ncb),
        grid=(nblk, n // bn),
        in_specs=[
            pl.BlockSpec((bm, d), lambda m, j: (jnp.minimum(m, ncb - 1), 0)),
            pl.BlockSpec((bm, d), lambda m, j: (jnp.maximum(m - ncb, 0), 0)),
            pl.BlockSpec((1, d), lambda m, j: (0, 0)),
            pl.BlockSpec((None, 1, d), lambda m, j: (seq_row(m) * 6 + 0, 0, 0)),
            pl.BlockSpec((None, 1, d), lambda m, j: (seq_row(m) * 6 + 1, 0, 0)),
            pl.BlockSpec((d, bn), lambda m, j: (0, j)),
        ],
        out_specs=out_spec,
        out_shape=out_shape,
        scratch_shapes=[pltpu.VMEM((bm, d), BF16)],
        compiler_params=_params("parallel", "arbitrary"),
        name="inproj",
    )(xp, xs, g, mods, mods, w)


def _hgrn_kernel(qf_ref, vf_ref, ff_ref, qb_ref, vb_ref, fb_ref, lb_ref, s0_ref, trf_ref, trb_ref, blk_ref,
                 of_ref, ob_ref, sfin_ref, st, qs_s, qt_s, kt_s, bl_s, cl_s, v_s, de_s, u_s, sb_s,
                 *, n_ctx_steps, lat_steps_per_seq):
    c = SCAN_CHUNK
    lc = qf_ref.shape[0]
    n_chunks = lc // c
    s = pl.program_id(1)
    is_ctx = s < n_ctx_steps
    j = jnp.where(is_ctx, 0, (s - n_ctx_steps) % lat_steps_per_seq)
    first = jnp.logical_or(is_ctx, j == 0)

    @pl.when(first)
    def _():
        for d in range(2):
            st[d] = jnp.where(is_ctx, 0.0, s0_ref[d].T)

    def split3_dot(a, g):
        g1 = g.astype(BF16)
        r1 = g - g1.astype(F32)
        g2 = r1.astype(BF16)
        g3 = (r1 - g2.astype(F32)).astype(BF16)
        dot = functools.partial(jnp.dot, preferred_element_type=F32)
        return dot(a, g1) + dot(a, g2) + dot(a, g3)

    log2e = 1.4426950408889634
    for d, (q_ref, v_ref, f_ref, tri_ref) in enumerate(((qf_ref, vf_ref, ff_ref, trf_ref),
                                                        (qb_ref, vb_ref, fb_ref, trb_ref))):
        qs = _silu(q_ref[...].astype(F32))
        lb = lb_ref[d]
        f = lb + (1.0 - lb) * jax.nn.sigmoid(f_ref[...].astype(F32))
        k = 1.0 - f
        g = jnp.log(f)
        b = split3_dot(tri_ref[...], g)
        b_end = split3_dot(blk_ref[...], g)
        qs_s[d] = qs
        qt_s[d] = qs * jnp.exp(b)
        kt_s[d] = k * jnp.exp(b_end - b)
        bl = b * log2e
        bl_s[d] = bl
        cl_s[d] = bl - jnp.log2(k)
        v_s[d] = v_ref[...].astype(F32)
        de_s[d] = jnp.exp(b_end)

    def chunk_of(d, ci):
        return ci if d == 0 else n_chunks - 1 - ci

    per_tile = LANES // c
    chunk_of_col = lax.broadcasted_iota(I32, (LANES, LANES), 1) // c
    for d in range(2):
        for tile in range(lc // LANES):
            rows = slice(tile * LANES, (tile + 1) * LANES)
            v_t = v_s[d, rows, :].T.astype(BF16)
            lhs = jnp.concatenate([jnp.where(chunk_of_col == ci, v_t, 0.0) for ci in range(per_tile)], axis=0)
            u_all = jnp.dot(lhs, kt_s[d, rows, :].astype(BF16), preferred_element_type=F32)
            for ci in range(per_tile):
                u_s[d, tile * per_tile + ci] = u_all[ci * LANES:(ci + 1) * LANES]

    def scan_step(ci, states):
        new = []
        for d in range(2):
            cidx = chunk_of(d, ci)
            sb_s[d, cidx] = states[d].astype(BF16)
            r0 = pl.multiple_of(cidx * c, c)
            new.append(states[d] * de_s[d, pl.ds(r0, 1), :] + u_s[d, cidx])
        return tuple(new)

    final = lax.fori_loop(0, n_chunks, scan_step, (st[0], st[1]))
    for d in range(2):
        st[d] = final[d]

    ones = jnp.ones((LANES, LANES), BF16)
    t_in_group = lax.broadcasted_iota(I32, (SUBLANES, LANES), 0)
    groups = c // SUBLANES
    per_trip = 4

    def out_step(i, carry):
        for w in range(per_trip):
            ci = i * per_trip + w
            rows = pl.ds(pl.multiple_of(ci * c, c), c)
            for d, o_ref in enumerate((of_ref, ob_ref)):
                qs, qt, bl, cl, v = (ref[d, rows, :] for ref in (qs_s, qt_s, bl_s, cl_s, v_s))
                o = lax.dot_general(qt.astype(BF16), sb_s[d, ci], (((1,), (1,)), ((), ())),
                                    preferred_element_type=F32)
                terms = []
                for si in range(c):
                    gs = si // SUBLANES
                    for gi in (range(gs, groups) if d == 0 else range(0, gs + 1)):
                        sl = slice(gi * SUBLANES, (gi + 1) * SUBLANES)
                        diff = bl[sl] - cl[si:si + 1]
                        if gi == gs:
                            t = t_in_group + gi * SUBLANES
                            diff = jnp.where((t >= si) if d == 0 else (t <= si), diff, -jnp.inf)
                        terms.append((si, gi, qs[sl] * jnp.exp2(diff)))
                stacked = jnp.concatenate([x for _, _, x in terms], axis=0).astype(BF16)
                score = jnp.dot(stacked, ones, preferred_element_type=F32)
                parts = [o[gi * SUBLANES:(gi + 1) * SUBLANES] for gi in range(groups)]
                for n, (si, gi, _) in enumerate(terms):
                    parts[gi] = parts[gi] + score[n * SUBLANES:(n + 1) * SUBLANES] * v[si:si + 1]
                o_ref[rows, :] = jnp.concatenate(parts, axis=0).astype(o_ref.dtype)
        return carry

    lax.fori_loop(0, n_chunks // per_trip, out_step, 0)

    @pl.when(is_ctx)
    def _():
        for d in range(2):
            sfin_ref[d] = st[d].T


def _hgrn(uh, lb, s0, lay):
    h = lay["heads"]
    lc = lay["lc"]
    ncs = lay["tc"] // lc
    spl = lay["lat_len"] // lc
    nls = lay["tl"] // lc
    nc = lc // SCAN_CHUNK
    assert lay["ctx_len"] == lc and nc % 4 == 0

    row = lax.broadcasted_iota(I32, (lc, lc), 0)
    col = lax.broadcasted_iota(I32, (lc, lc), 1)
    same = (row // SCAN_CHUNK) == (col // SCAN_CHUNK)
    tri_f = jnp.logical_and(same, col <= row).astype(BF16)
    tri_b = jnp.logical_and(same, col >= row).astype(BF16)

    def fwd_blk(s):
        return s

    def bwd_blk(s):
        q = s - ncs
        return jnp.where(s < ncs, s, ncs + (q // spl) * spl + (spl - 1 - q % spl))

    def lat_seq(s):
        return jnp.maximum(s - ncs, 0) // spl

    def part(p, blk):
        return pl.BlockSpec((None, lc, LANES), lambda hh, s: (p * h + hh, blk(s), 0))

    const = pl.BlockSpec((lc, lc), lambda hh, s: (0, 0))
    o_spec_f = pl.BlockSpec((None, lc, LANES), lambda hh, s: (hh, fwd_blk(s), 0))
    o_spec_b = pl.BlockSpec((None, lc, LANES), lambda hh, s: (hh, bwd_blk(s), 0))
    state_shape = (lay["ctx_batch"], 2, h, LANES, LANES)
    sfin_spec = pl.BlockSpec((None, 2, None, LANES, LANES), lambda hh, s: (jnp.minimum(s, ncs - 1), 0, hh, 0, 0))
    scr = pltpu.VMEM((2, lc, LANES), F32)
    return pl.pallas_call(
        functools.partial(_hgrn_kernel, n_ctx_steps=ncs, lat_steps_per_seq=spl),
        grid=(h, ncs + nls),
        in_specs=[
            part(0, fwd_blk), part(1, fwd_blk), part(2, fwd_blk),
            part(0, bwd_blk), part(1, bwd_blk), part(3, bwd_blk),
            pl.BlockSpec((2, None, 1, LANES), lambda hh, s: (0, hh, 0, 0)),
            pl.BlockSpec((None, 2, None, LANES, LANES), lambda hh, s: (lat_seq(s), 0, hh, 0, 0)),
            const, const, const,
        ],
        out_specs=[o_spec_f, o_spec_b, sfin_spec],
        out_shape=[
            jax.ShapeDtypeStruct((h, lay["t"], LANES), BF16),
            jax.ShapeDtypeStruct((h, lay["t"], LANES), BF16),
            jax.ShapeDtypeStruct(state_shape, F32),
        ],
        scratch_shapes=[pltpu.VMEM((2, LANES, LANES), F32)] + [scr] * 7 + [
            pltpu.VMEM((2, nc, LANES, LANES), F32), pltpu.VMEM((2, nc, LANES, LANES), BF16)],
        compiler_params=_params("arbitrary", "arbitrary"),
        name="hgrn_scan",
    )(uh, uh, uh, uh, uh, uh, lb, s0, tri_f, tri_b, same.astype(BF16))


def _branches_kernel(*refs, n_groups, n_ctx_blocks, seg_ctx, seg_lat, heads, d_a, d_b, d_model):
    of_ref, ob_ref = refs[:2]
    u_refs = refs[2:2 + n_groups]
    hg_ref, cw_ref, cb_ref, lng_ref, lnb_ref, wa_ref, wb_ref, o_ref, pad_c, pad_l, cv = refs[2 + n_groups:]
    bm = o_ref.shape[0]
    m = pl.program_id(0)
    taps = cw_ref.shape[0]
    half = taps // 2

    def u_tile(col):
        tile = col // LANES
        return u_refs[tile // heads][tile % heads].astype(F32)

    def u_cols(col, width):
        return jnp.concatenate([u_tile(col + i * LANES) for i in range(width // LANES)], axis=1)

    o_h = [of_ref[h].astype(F32) + ob_ref[h].astype(F32) for h in range(heads)]
    ss = o_h[0] * o_h[0]
    for h in range(1, heads):
        ss = ss + o_h[h] * o_h[h]
    inv = lax.rsqrt(jnp.sum(ss, axis=-1, keepdims=True) / d_a + EPS)
    gated = []
    for h in range(heads):
        sl = slice(h * LANES, (h + 1) * LANES)
        gated.append((o_h[h] * inv * hg_ref[:, sl]) * _silu(u_tile(h * LANES)))
    y_a = jnp.dot(jnp.concatenate(gated, axis=1).astype(BF16), wa_ref[...], preferred_element_type=F32)

    glu = u_cols(d_a, d_b) * jax.nn.sigmoid(u_cols(d_a + d_b, d_b))

    def conv(pad, seg):
        stride = seg + 2 * CONV_PAD
        n_seg = bm // seg
        zeros = jnp.zeros((CONV_PAD, d_b), F32)
        for i in range(n_seg):
            pad[i * stride:i * stride + CONV_PAD, :] = zeros
            pad[i * stride + CONV_PAD:i * stride + CONV_PAD + seg, :] = glu[i * seg:(i + 1) * seg]
            pad[i * stride + CONV_PAD + seg:(i + 1) * stride, :] = zeros
        rows = min(seg, 64)
        cols = 256
        for i in range(n_seg):
            for r in range(seg // rows):
                base = i * stride + CONV_PAD + r * rows - half

                def col_step(ci, carry, base=base, i=i, r=r):
                    c0 = pl.multiple_of(ci * cols, cols)
                    acc = jnp.zeros((rows, cols), F32)
                    for j in range(taps):
                        acc = acc + pad[base + j:base + j + rows, pl.ds(c0, cols)] * cw_ref[j:j + 1, pl.ds(c0, cols)]
                    cv[i * seg + r * rows:i * seg + (r + 1) * rows, pl.ds(c0, cols)] = acc
                    return carry

                lax.fori_loop(0, d_b // cols, col_step, 0)

    @pl.when(m < n_ctx_blocks)
    def _():
        conv(pad_c, seg_ctx)

    @pl.when(m >= n_ctx_blocks)
    def _():
        conv(pad_l, seg_lat)

    v = cv[...] + cb_ref[...]
    mu = jnp.mean(v, axis=-1, keepdims=True)
    var = jnp.mean(jnp.square(v - mu), axis=-1, keepdims=True)
    v = (v - mu) * lax.rsqrt(var + EPS) * lng_ref[...] + lnb_ref[...]
    y_b = jnp.dot(_silu(v).astype(BF16), wb_ref[...], preferred_element_type=F32)

    g0 = d_a + 2 * d_b
    g_a = jax.nn.sigmoid(u_cols(g0, d_model))
    g_b = jax.nn.sigmoid(u_cols(g0 + d_model, d_model))
    o_ref[...] = (g_a * y_a + g_b * y_b).astype(o_ref.dtype)


def _branches(o_f, o_b, u, hg, cw, cb, lng, lnb, wa, wb, lay):
    heads, d_a, d_b, d = lay["heads"], lay["d_a"], lay["d_b"], lay["d"]
    bm = lay["bm3"]
    ncb = lay["tc"] // bm
    seg_c = min(lay["ctx_len"], bm)
    seg_l = GRID_W
    taps = cw.shape[0]
    const = lambda m: (0, 0)
    resident = dict(pipeline_mode=pl.Buffered(1))
    assert d_b % d_a == 0 and d % d_a == 0
    n_groups = (d_a + 2 * d_b + 2 * d) // d_a

    def pad_rows(seg):
        return (bm // seg) * (seg + 2 * CONV_PAD)

    def group(i):
        return pl.BlockSpec((heads, bm, LANES), lambda m: (4 + i, m, 0))

    return pl.pallas_call(
        functools.partial(_branches_kernel, n_groups=n_groups, n_ctx_blocks=ncb, seg_ctx=seg_c, seg_lat=seg_l,
                          heads=heads, d_a=d_a, d_b=d_b, d_model=d),
        grid=(lay["t"] // bm,),
        in_specs=[
            pl.BlockSpec((heads, bm, LANES), lambda m: (0, m, 0)),
            pl.BlockSpec((heads, bm, LANES), lambda m: (0, m, 0)),
            *[group(i) for i in range(n_groups)],
            pl.BlockSpec((1, d_a), const),
            pl.BlockSpec((taps, d_b), const),
            pl.BlockSpec((1, d_b), const),
            pl.BlockSpec((1, d_b), const),
            pl.BlockSpec((1, d_b), const),
            pl.BlockSpec((d_a, d), const, **resident),
            pl.BlockSpec((d_b, d), const, **resident),
        ],
        out_specs=pl.BlockSpec((bm, d), lambda m: (m, 0)),
        out_shape=jax.ShapeDtypeStruct((lay["t"], d), BF16),
        scratch_shapes=[
            pltpu.VMEM((pad_rows(seg_c), d_b), F32),
            pltpu.VMEM((pad_rows(seg_l), d_b), F32),
            pltpu.VMEM((bm, d_b), F32),
        ],
        compiler_params=_params("parallel"),
        name="branches_merge",
    )(o_f, o_b, *([u] * n_groups), hg, cw, cb, lng, lnb, wa, wb)


def _pack_bf16_pairs(lo, hi):
    lo_bits = lax.bitcast_convert_type(lo, U32)
    hi_bits = lax.bitcast_convert_type(hi, U32)
    return (lo_bits >> 16) | (hi_bits & jnp.uint32(0xFFFF0000))


def _unpack_bf16_pairs(packed):
    lo = lax.bitcast_convert_type(packed << 16, F32)
    hi = lax.bitcast_convert_type(packed & jnp.uint32(0xFFFF0000), F32)
    return lo, hi


def _outproj_kernel(mix_ref, xp_ref, xs_ref, g1_ref, ng_ref, sh2_ref, sc2_ref, wo_ref, wr_ref, br_ref,
                    x1_ref, hp_ref, rt_ref, cnt_ref, *, n_ctx_blocks, n_experts):
    m = pl.program_id(0)

    @pl.when(m == 0)
    def _():
        cnt_ref[...] = jnp.zeros(cnt_ref.shape, F32)

    x = jnp.where(m < n_ctx_blocks, xp_ref[...], xs_ref[...])
    y = jnp.dot(mix_ref[...], wo_ref[...], preferred_element_type=F32)
    x1 = x + g1_ref[...] * y
    x1_ref[...] = x1
    hb = _mod_rmsnorm(x1, ng_ref[...], sc2_ref[...], sh2_ref[...]).astype(BF16)
    hf = hb.astype(F32)
    half = hf.shape[1] // 2
    hp_ref[...] = _pack_bf16_pairs(hf[:, :half], hf[:, half:])

    logits = jnp.dot(hb, wr_ref[...], preferred_element_type=F32) + br_ref[...]
    lane = lax.broadcasted_iota(I32, logits.shape, 1).astype(F32)
    logits = jnp.where(lane < n_experts, logits, -jnp.inf)
    vals, idxs = [], []
    for _ in range(TOP_K):
        mx = jnp.max(logits, axis=-1, keepdims=True)
        ix = jnp.min(jnp.where(logits == mx, lane, float(LANES)), axis=-1, keepdims=True)
        vals.append(mx)
        idxs.append(ix)
        logits = jnp.where(lane == ix, -jnp.inf, logits)
    es = [jnp.exp(v - vals[0]) for v in vals]
    tot = es[0]
    for e in es[1:]:
        tot = tot + e
    hot = [(lane == ix).astype(F32) for ix in idxs]
    hot_all = hot[0] + hot[1] + hot[2] + hot[3]
    bm = hot_all.shape[0]
    earlier = (lax.broadcasted_iota(I32, (bm, bm), 1) < lax.broadcasted_iota(I32, (bm, bm), 0)).astype(BF16)
    before = jnp.dot(earlier, hot_all.astype(BF16), preferred_element_type=F32) + cnt_ref[...]
    out = jnp.zeros(logits.shape, F32)
    for k in range(TOP_K):
        out = jnp.where(lane == k, idxs[k], out)
        out = jnp.where(lane == TOP_K + k, es[k] / tot, out)
        out = jnp.where(lane == 2 * TOP_K + k, jnp.sum(before * hot[k], axis=-1, keepdims=True), out)
    rt_ref[...] = out
    cnt_ref[...] = before[bm - 1:bm, :] + hot_all[bm - 1:bm, :]


def _outproj(mix, xp, xs, mods, ng, wo, wr, br, lay):
    d = lay["d"]
    bm = lay["bm3"]
    ncb = lay["tc"] // bm
    t = lay["t"]

    def seq_row(m):
        return jnp.where(m < ncb, lay["ctx_row"], ((m - ncb) * bm) // lay["lat_len"])

    const = lambda m: (0, 0)
    mod = lambda part: pl.BlockSpec((None, 1, d), lambda m: (seq_row(m) * 6 + part, 0, 0))
    return pl.pallas_call(
        functools.partial(_outproj_kernel, n_ctx_blocks=ncb, n_experts=lay["experts"]),
        grid=(t // bm,),
        in_specs=[
            pl.BlockSpec((bm, d), lambda m: (m, 0)),
            pl.BlockSpec((bm, d), lambda m: (jnp.minimum(m, ncb - 1), 0)),
            pl.BlockSpec((bm, d), lambda m: (jnp.maximum(m - ncb, 0), 0)),
            mod(2),
            pl.BlockSpec((1, d), const),
            mod(3),
            mod(4),
            pl.BlockSpec((d, d), const, pipeline_mode=pl.Buffered(1)),
            pl.BlockSpec((d, LANES), const),
            pl.BlockSpec((1, LANES), const),
        ],
        out_specs=[
            pl.BlockSpec((bm, d), lambda m: (m, 0)),
            pl.BlockSpec((bm, d // 2), lambda m: (m, 0)),
            pl.BlockSpec((bm, LANES), lambda m: (m, 0)),
            pl.BlockSpec((1, LANES), const),
        ],
        out_shape=[
            jax.ShapeDtypeStruct((t, d), F32),
            jax.ShapeDtypeStruct((t, d // 2), U32),
            jax.ShapeDtypeStruct((t, LANES), F32),
            jax.ShapeDtypeStruct((1, LANES), F32),
        ],
        compiler_params=_params("arbitrary"),
        name="outproj_router",
    )(mix, xp, xs, mods, ng, mods, mods, wo, wr, br)


def _routing_tables(rt, counts_row, lay):
    e, bm = lay["experts"], lay["bm_moe"]
    nb = lay["moe_blocks"]
    idx = rt[:, :TOP_K].astype(I32)
    rank = rt[:, 2 * TOP_K:3 * TOP_K].astype(I32)
    counts = counts_row[0, :e].astype(I32)
    nblk = (counts + bm - 1) // bm
    blk_end = jnp.cumsum(nblk).astype(I32)
    blk_start = blk_end - nblk
    experts = jnp.arange(e, dtype=I32)
    first_slot = jnp.sum(jnp.where(idx[:, :, None] == experts, blk_start * bm, 0), axis=-1)
    pos = first_slot + rank
    n_real = blk_end[-1]

    def lookup(table, i):
        return jnp.sum(jnp.where(i[:, None] == experts, table, 0), axis=-1)

    def schedule(nt):
        s = jnp.arange(nb * nt, dtype=I32)
        ex = jnp.minimum(jnp.sum((blk_end * nt <= s[:, None]).astype(I32), axis=-1), e - 1)
        r = s - lookup(blk_start, ex) * nt
        per = jnp.maximum(lookup(nblk, ex), 1)
        real = s < n_real * nt
        tail = s - n_real * nt
        m_idx = jnp.where(real, lookup(blk_start, ex) + r % per, n_real + tail // nt)
        n_out = jnp.where(real, r // per, tail % nt)
        is_last = s == jnp.maximum(n_real * nt - 1, 0)
        e_last = jnp.sum(jnp.where(is_last, ex, 0))
        n_last = jnp.sum(jnp.where(is_last, r // per, 0))
        e_w = jnp.where(real, ex, e_last)
        n_w = jnp.where(real, n_out, n_last)
        flag = jnp.where(real, 1 + 2 * (r % per == 0).astype(I32), 0)
        return m_idx.astype(I32), n_out.astype(I32), e_w.astype(I32), n_w.astype(I32), flag.astype(I32)

    return pos, schedule(lay["nt_up"]), schedule(lay["nt_down"])


DMA_UNROLL = 8


def _dispatch_kernel(pos_ref, h_ref, zero_ref, out_ref, sem):
    del zero_ref
    bt = h_ref.shape[0]

    def issue(i, carry):
        for u in range(DMA_UNROLL):
            t = i * DMA_UNROLL + u
            for k in range(TOP_K):
                slot = pos_ref[0, k * bt + t]
                pltpu.make_async_copy(h_ref.at[pl.ds(t, 1), :], out_ref.at[pl.ds(slot, 1), :], sem).start()
        return carry

    lax.fori_loop(0, bt // DMA_UNROLL, issue, 0)
    n = TOP_K * bt
    pltpu.make_async_copy(out_ref.at[pl.ds(0, n), :], out_ref.at[pl.ds(0, n), :], sem).wait()


def _dispatch_rows(hp, pos_blocks, lay):
    bt = lay["bt"]
    t, width = hp.shape
    rows = lay["moe_blocks"] * lay["bm_moe"]
    return pl.pallas_call(
        _dispatch_kernel,
        grid=(t // bt,),
        in_specs=[
            pl.BlockSpec((None, 1, TOP_K * bt), lambda m: (m, 0, 0), memory_space=pltpu.SMEM),
            pl.BlockSpec((bt, width), lambda m: (m, 0)),
            pl.BlockSpec(memory_space=pl.ANY),
        ],
        out_specs=pl.BlockSpec(memory_space=pl.ANY),
        out_shape=jax.ShapeDtypeStruct((rows, width), hp.dtype),
        scratch_shapes=[pltpu.SemaphoreType.DMA(())],
        input_output_aliases={2: 0},
        compiler_params=_params("arbitrary"),
        name="moe_dispatch",
    )(pos_blocks, hp, jnp.zeros((rows, width), hp.dtype))


def _up_kernel(sm, sn, se, sw, sf, x_ref, wg_ref, wu_ref, bg_ref, bu_ref, o_ref, wg_bf, wu_bf):
    flag = sf[pl.program_id(0)]

    @pl.when(flag >= 2)
    def _():
        wg_bf[...] = wg_ref[...].astype(BF16)
        wu_bf[...] = wu_ref[...].astype(BF16)

    @pl.when(flag >= 1)
    def _():
        lo, hi = _unpack_bf16_pairs(x_ref[...])
        x = jnp.concatenate([lo.astype(BF16), hi.astype(BF16)], axis=1)
        gate = jnp.dot(x, wg_bf[...], preferred_element_type=F32) + bg_ref[...]
        up = jnp.dot(x, wu_bf[...], preferred_element_type=F32) + bu_ref[...]
        gate = jnp.minimum(gate, SWIGLU_LIMIT)
        up = jnp.clip(up, -SWIGLU_LIMIT, SWIGLU_LIMIT)
        o_ref[...] = ((up + 1.0) * gate * jax.nn.sigmoid(SWIGLU_ALPHA * gate)).astype(o_ref.dtype)

    @pl.when(flag == 0)
    def _():
        o_ref[...] = jnp.zeros(o_ref.shape, o_ref.dtype)


def _moe_up(xs, w_gu, b_gu, sched, lay):
    bm, nb, nt = lay["bm_moe"], lay["moe_blocks"], lay["nt_up"]
    d, ff = lay["d"], lay["d_ff"]
    tn = ff // nt
    e = lay["experts"]
    grid_spec = pltpu.PrefetchScalarGridSpec(
        num_scalar_prefetch=5,
        grid=(nb * nt,),
        in_specs=[
            pl.BlockSpec((bm, d // 2), lambda s, sm, sn, se, sw, sf: (sm[s], 0)),
            pl.BlockSpec((None, d, tn), lambda s, sm, sn, se, sw, sf: (se[s], 0, sw[s])),
            pl.BlockSpec((None, d, tn), lambda s, sm, sn, se, sw, sf: (se[s], 0, nt + sw[s])),
            pl.BlockSpec((None, 1, tn), lambda s, sm, sn, se, sw, sf: (se[s], 0, sw[s])),
            pl.BlockSpec((None, 1, tn), lambda s, sm, sn, se, sw, sf: (se[s], 0, nt + sw[s])),
        ],
        out_specs=pl.BlockSpec((bm, tn), lambda s, sm, sn, se, sw, sf: (sm[s], sn[s])),
        scratch_shapes=[pltpu.VMEM((d, tn), BF16), pltpu.VMEM((d, tn), BF16)],
    )
    return pl.pallas_call(
        _up_kernel,
        grid_spec=grid_spec,
        out_shape=jax.ShapeDtypeStruct((nb * bm, ff), BF16),
        compiler_params=_params("arbitrary"),
        name="moe_gate_up",
    )(*sched, xs, w_gu, w_gu, b_gu.reshape(e, 1, 2 * ff), b_gu.reshape(e, 1, 2 * ff))


def _down_kernel(sm, sn, se, sw, sf, a_ref, wl_ref, wh_ref, bl_ref, bh_ref, o_ref, wl_bf, wh_bf):
    flag = sf[pl.program_id(0)]

    @pl.when(flag >= 2)
    def _():
        wl_bf[...] = wl_ref[...].astype(BF16)
        wh_bf[...] = wh_ref[...].astype(BF16)

    @pl.when(flag >= 1)
    def _():
        a = a_ref[...]
        lo = jnp.dot(a, wl_bf[...], preferred_element_type=F32) + bl_ref[...]
        hi = jnp.dot(a, wh_bf[...], preferred_element_type=F32) + bh_ref[...]
        o_ref[...] = _pack_bf16_pairs(lo.astype(BF16).astype(F32), hi.astype(BF16).astype(F32))

    @pl.when(flag == 0)
    def _():
        o_ref[...] = jnp.zeros(o_ref.shape, o_ref.dtype)


def _moe_down(act, w_dn, b_dn, sched, lay):
    bm, nb, nt = lay["bm_moe"], lay["moe_blocks"], lay["nt_down"]
    d, ff = lay["d"], lay["d_ff"]
    half = d // 2
    tn = half // nt
    e = lay["experts"]
    grid_spec = pltpu.PrefetchScalarGridSpec(
        num_scalar_prefetch=5,
        grid=(nb * nt,),
        in_specs=[
            pl.BlockSpec((bm, ff), lambda s, sm, sn, se, sw, sf: (sm[s], 0)),
            pl.BlockSpec((None, ff, tn), lambda s, sm, sn, se, sw, sf: (se[s], 0, sw[s])),
            pl.BlockSpec((None, ff, tn), lambda s, sm, sn, se, sw, sf: (se[s], 0, nt + sw[s])),
            pl.BlockSpec((None, 1, tn), lambda s, sm, sn, se, sw, sf: (se[s], 0, sw[s])),
            pl.BlockSpec((None, 1, tn), lambda s, sm, sn, se, sw, sf: (se[s], 0, nt + sw[s])),
        ],
        out_specs=pl.BlockSpec((bm, tn), lambda s, sm, sn, se, sw, sf: (sm[s], sn[s])),
        scratch_shapes=[pltpu.VMEM((ff, tn), BF16), pltpu.VMEM((ff, tn), BF16)],
    )
    return pl.pallas_call(
        _down_kernel,
        grid_spec=grid_spec,
        out_shape=jax.ShapeDtypeStruct((nb * bm, half), U32),
        compiler_params=_params("arbitrary"),
        name="moe_down",
    )(*sched, act, w_dn, w_dn, b_dn.reshape(e, 1, d), b_dn.reshape(e, 1, d))


def _combine_kernel(pos_ref, y_ref, rt_ref, x1_ref, g2_ref, fg_ref, o_ref, buf, sem):
    bt = o_ref.shape[0]
    n = pos_ref.shape[1]

    def issue(i, carry):
        for u in range(DMA_UNROLL):
            r = i * DMA_UNROLL + u
            pltpu.make_async_copy(y_ref.at[pl.ds(pos_ref[0, r], 1), :], buf.at[pl.ds(r, 1), :], sem).start()
        return carry

    lax.fori_loop(0, n // DMA_UNROLL, issue, 0)
    pltpu.make_async_copy(y_ref.at[pl.ds(0, n), :], buf, sem).wait()
    lo = hi = None
    for k in range(TOP_K):
        p = rt_ref[:, TOP_K + k:TOP_K + k + 1]
        lo_k, hi_k = _unpack_bf16_pairs(buf[k * bt:(k + 1) * bt, :])
        lo = lo_k * p if lo is None else lo + lo_k * p
        hi = hi_k * p if hi is None else hi + hi_k * p
    x2 = x1_ref[...] + g2_ref[...] * jnp.concatenate([lo, hi], axis=1)
    o_ref[...] = x2 * lax.rsqrt(jnp.mean(x2 * x2, axis=-1, keepdims=True) + EPS) * fg_ref[...]


def _combine(y_sorted, pos, rt, x1, mods, fg, lay, *, blk0, n_tok, seq_len, row0):
    d = lay["d"]
    bt = lay["bt"]
    nblk = n_tok // bt

    def seq_row(m):
        return row0 if seq_len is None else row0 + (m * bt) // seq_len

    return pl.pallas_call(
        _combine_kernel,
        grid=(nblk,),
        in_specs=[
            pl.BlockSpec((None, 1, TOP_K * bt), lambda m: (blk0 + m, 0, 0), memory_space=pltpu.SMEM),
            pl.BlockSpec(memory_space=pl.ANY),
            pl.BlockSpec((bt, LANES), lambda m: (blk0 + m, 0)),
            pl.BlockSpec((bt, d), lambda m: (blk0 + m, 0)),
            pl.BlockSpec((None, 1, d), lambda m: (seq_row(m) * 6 + 5, 0, 0)),
            pl.BlockSpec((1, d), lambda m: (0, 0)),
        ],
        out_specs=pl.BlockSpec((bt, d), lambda m: (m, 0)),
        out_shape=jax.ShapeDtypeStruct((n_tok, d), F32),
        scratch_shapes=[pltpu.VMEM((TOP_K * bt, d // 2), U32), pltpu.SemaphoreType.DMA(())],
        compiler_params=_params("arbitrary"),
        name="moe_combine_final",
    )(pos, y_sorted, rt, x1, mods, fg)


def _layout(x_prompt, x_sample, state_hgrn, w_gate_up):
    cb, cl, d = x_prompt.shape
    lb_, ll, _ = x_sample.shape
    heads = state_hgrn.shape[3]
    experts, _, ff2 = w_gate_up.shape[1:]
    tc, tl = cb * cl, lb_ * ll
    t = tc + tl
    bm_moe = 512 if t * TOP_K >= 32768 else 128
    lay = dict(
        d=d, t=t, tc=tc, tl=tl, ctx_batch=cb, ctx_len=cl, lat_batch=lb_, lat_len=ll, ctx_row=lb_,
        heads=heads, d_a=heads * LANES, experts=experts, d_ff=ff2 // 2,
        bm1=512 if (tc % 512 == 0 and ll % 512 == 0) else cl, bn1=min(1024, heads * LANES), lc=cl, bm3=cl,
        bt=min(256, cl),
        bm_moe=bm_moe, moe_blocks=pl.cdiv(t * TOP_K, bm_moe) + experts,
        nt_up=max(1, (ff2 // 2) // 512), nt_down=max(1, (d // 2) // 512),
    )
    assert state_hgrn.shape[4] == LANES and state_hgrn.shape[5] == LANES
    assert tc % lay["bm1"] == 0 and tl % lay["bm1"] == 0 and ll % lay["bm1"] == 0
    assert ll % cl == 0 and cl % GRID_W == 0 and cl % SCAN_CHUNK == 0
    return lay


def kernel(x_prompt, x_sample, state_hgrn, c, c_ctx, w_ada, b_ada, norm_mix_g, w_in, lb_logits, hgrn_norm_g, w_a_out, conv_w, conv_b, conv_ln_g, conv_ln_b, w_b_out, w_out, norm_moe_g, w_router, b_router, w_gate_up, b_gate_up, w_down, b_down, final_norm_g):
    assert w_ada.shape[0] == 1, "one trunk layer"
    lay = _layout(x_prompt, x_sample, state_hgrn, w_gate_up)
    d, d_a, heads = lay["d"], lay["d_a"], lay["heads"]
    d_b = conv_w.shape[2]
    lay["d_b"] = d_b
    xp = x_prompt.reshape(lay["tc"], d)
    xs = x_sample.reshape(lay["tl"], d)

    n_rows = lay["lat_batch"] + 1
    rows_pad = pl.cdiv(n_rows, SUBLANES) * SUBLANES
    cond = jnp.concatenate([c, c_ctx[None, :], jnp.zeros((rows_pad - n_rows, d), F32)], axis=0)
    mods = _adaln(cond, w_ada.reshape(d, 6 * d), b_ada.reshape(1, 6 * d)).reshape(rows_pad * 6, 1, d)

    w_in_bf = w_in.reshape(d, -1).astype(BF16)
    u = _inproj(xp, xs, norm_mix_g.reshape(1, d), mods, w_in_bf, lay)

    lb_all = jnp.cumsum(jax.nn.softmax(lb_logits.astype(F32), axis=0), axis=0)[0]
    s0 = state_hgrn.reshape(lay["lat_batch"], 2, heads, LANES, LANES)
    o_f, o_b, s_fin = _hgrn(u, lb_all.reshape(2, heads, 1, LANES), s0, lay)

    mix = _branches(o_f, o_b, u, hgrn_norm_g.reshape(1, d_a), conv_w.reshape(-1, d_b), conv_b.reshape(1, d_b),
                    conv_ln_g.reshape(1, d_b), conv_ln_b.reshape(1, d_b), w_a_out.reshape(d_a, d).astype(BF16),
                    w_b_out.reshape(d_b, d).astype(BF16), lay)

    e, ff = lay["experts"], lay["d_ff"]
    wr = jnp.zeros((d, LANES), BF16).at[:, :e].set(w_router.reshape(d, e).astype(BF16))
    br = jnp.zeros((1, LANES), F32).at[:, :e].set(b_router.reshape(1, e))
    x1, hp, rt, counts = _outproj(mix, xp, xs, mods, norm_moe_g.reshape(1, d), w_out.reshape(d, d).astype(BF16), wr,
                                  br, lay)

    pos, sched_up, sched_down = _routing_tables(rt, counts, lay)
    bt = lay["bt"]
    pos_blocks = pos.reshape(lay["t"] // bt, bt, TOP_K).transpose(0, 2, 1).reshape(lay["t"] // bt, 1, TOP_K * bt)
    x_sorted = _dispatch_rows(hp, pos_blocks, lay)
    act = _moe_up(x_sorted, w_gate_up.reshape(e, d, 2 * ff), b_gate_up.reshape(e, 2 * ff), sched_up, lay)
    y_sorted = _moe_down(act, w_down.reshape(e, ff, d), b_down.reshape(e, d), sched_down, lay)

    fg = final_norm_g.reshape(1, d)
    y_prompt = _combine(y_sorted, pos_blocks, rt, x1, mods, fg, lay, blk0=0, n_tok=lay["tc"],
                        seq_len=None, row0=lay["ctx_row"])
    y_sample = _combine(y_sorted, pos_blocks, rt, x1, mods, fg, lay, blk0=lay["tc"] // bt, n_tok=lay["tl"],
                        seq_len=lay["lat_len"], row0=0)
    return (y_prompt.reshape(x_prompt.shape), y_sample.reshape(x_sample.shape),
            s_fin.reshape(lay["ctx_batch"], 1, 2, heads, LANES, LANES))
```

```python
import functools

import jax
import jax.numpy as jnp
from jax import lax
from jax.experimental import pallas as pl
from jax.experimental.pallas import tpu as pltpu

F32 = jnp.float32
BF16 = jnp.bfloat16
U32 = jnp.uint32
I32 = jnp.int32

EPS = 1e-6
TOP_K = 4
GRID_W = 64
SWIGLU_LIMIT = 7.0
SWIGLU_ALPHA = 1.702

LANES = 128
SUBLANES = 8
SCAN_CHUNK = 16
CONV_PAD = 16
V7X_VMEM_LIMIT = 56 * 2**20


def _params(*sem):
    return pltpu.CompilerParams(dimension_semantics=sem, vmem_limit_bytes=V7X_VMEM_LIMIT)


def _tile(n, target):
    t = min(n, target) // LANES * LANES
    while n % t:
        t -= LANES
    return t


def _silu(x):
    return x * jax.nn.sigmoid(x)


def _mod_rmsnorm(x, g, scale, shift):
    y = x * lax.rsqrt(jnp.mean(x * x, axis=-1, keepdims=True) + EPS) * g
    return y * (1.0 + scale) + shift


def _ada_kernel(c_ref, w_ref, b_ref, o_ref):
    s = _silu(c_ref[...]).astype(BF16)
    o_ref[...] = jnp.dot(s, w_ref[...].astype(BF16), preferred_element_type=F32) + b_ref[...]


def _adaln(cond, w, b):
    rows, d = cond.shape
    n = w.shape[1]
    tn = _tile(n, 1024)
    return pl.pallas_call(
        _ada_kernel,
        grid=(n // tn,),
        in_specs=[
            pl.BlockSpec((rows, d), lambda j: (0, 0)),
            pl.BlockSpec((d, tn), lambda j: (0, j)),
            pl.BlockSpec((1, tn), lambda j: (0, j)),
        ],
        out_specs=pl.BlockSpec((rows, tn), lambda j: (0, j)),
        out_shape=jax.ShapeDtypeStruct((rows, n), F32),
        compiler_params=_params("arbitrary"),
        name="adaln",
    )(cond, w, b)


def _inproj_kernel(xp_ref, xs_ref, g_ref, sh_ref, sc_ref, w_ref, o_ref, h_scr, *, n_ctx_blocks):
    m = pl.program_id(0)

    @pl.when(pl.program_id(1) == 0)
    def _():
        x = jnp.where(m < n_ctx_blocks, xp_ref[...], xs_ref[...])
        h_scr[...] = _mod_rmsnorm(x, g_ref[...], sc_ref[...], sh_ref[...]).astype(BF16)

    acc = jnp.dot(h_scr[...], w_ref[...], preferred_element_type=F32)
    for j in range(o_ref.shape[0]):
        o_ref[j] = acc[:, j * LANES:(j + 1) * LANES].astype(o_ref.dtype)


def _inproj(xp, xs, g, mods, w, lay):
    d = xp.shape[1]
    n = w.shape[1]
    bm, bn = lay["bm1"], _tile(n, lay["bn1"])
    ncb = lay["tc"] // bm
    nblk = lay["t"] // bm

    def seq_row(m):
        return jnp.where(m < ncb, lay["ctx_row"], ((m - ncb) * bm) // lay["lat_len"])

    out_shape = jax.ShapeDtypeStruct((n // LANES, lay["t"], LANES), BF16)
    out_spec = pl.BlockSpec((bn // LANES, bm, LANES), lambda m, j: (j, m, 0))
    return pl.pallas_call(
        functools.partial(_inproj_kernel, n_ctx_blocks=ncb),
        grid=(nblk, n // bn),
        in_specs=[
            pl.BlockSpec((bm, d), lambda m, j: (jnp.minimum(m, ncb - 1), 0)),
            pl.BlockSpec((bm, d), lambda m, j: (jnp.maximum(m - ncb, 0), 0)),
            pl.BlockSpec((1, d), lambda m, j: (0, 0)),
            pl.BlockSpec((None, 1, d), lambda m, j: (seq_row(m) * 6 + 0, 0, 0)),
            pl.BlockSpec((None, 1, d), lambda m, j: (seq_row(m) * 6 + 1, 0, 0)),
            pl.BlockSpec((d, bn), lambda m, j: (0, j)),
        ],
        out_specs=out_spec,
        out_shape=out_shape,
        scratch_shapes=[pltpu.VMEM((bm, d), BF16)],
        compiler_params=_params("parallel", "arbitrary"),
        name="inproj",
    )(xp, xs, g, mods, mods, w)


def _hgrn_kernel(qf_ref, vf_ref, ff_ref, qb_ref, vb_ref, fb_ref, lb_ref, s0_ref, trf_ref, trb_ref, blk_ref,
                 of_ref, ob_ref, sfin_ref, st, qs_s, qt_s, kt_s, bl_s, cl_s, v_s, de_s, u_s, sb_s,
                 *, n_ctx_steps, lat_steps_per_seq):
    c = SCAN_CHUNK
    lc = qf_ref.shape[0]
    n_chunks = lc // c
    s = pl.program_id(1)
    is_ctx = s < n_ctx_steps
    j = jnp.where(is_ctx, 0, (s - n_ctx_steps) % lat_steps_per_seq)
    first = jnp.logical_or(is_ctx, j == 0)

    @pl.when(first)
    def _():
        for d in range(2):
            st[d] = jnp.where(is_ctx, 0.0, s0_ref[d].T)

    def split3_dot(a, g):
        g1 = g.astype(BF16)
        r1 = g - g1.astype(F32)
        g2 = r1.astype(BF16)
        g3 = (r1 - g2.astype(F32)).astype(BF16)
        dot = functools.partial(jnp.dot, preferred_element_type=F32)
        return dot(a, g1) + dot(a, g2) + dot(a, g3)

    log2e = 1.4426950408889634
    for d, (q_ref, v_ref, f_ref, tri_ref) in enumerate(((qf_ref, vf_ref, ff_ref, trf_ref),
                                                        (qb_ref, vb_ref, fb_ref, trb_ref))):
        qs = _silu(q_ref[...].astype(F32))
        lb = lb_ref[d]
        f = lb + (1.0 - lb) * jax.nn.sigmoid(f_ref[...].astype(F32))
        k = 1.0 - f
        g = jnp.log(f)
        b = split3_dot(tri_ref[...], g)
        b_end = split3_dot(blk_ref[...], g)
        qs_s[d] = qs
        qt_s[d] = qs * jnp.exp(b)
        kt_s[d] = k * jnp.exp(b_end - b)
        bl = b * log2e
        bl_s[d] = bl
        cl_s[d] = bl - jnp.log2(k)
        v_s[d] = v_ref[...].astype(F32)
        de_s[d] = jnp.exp(b_end)

    def chunk_of(d, ci):
        return ci if d == 0 else n_chunks - 1 - ci

    per_tile = LANES // c
    chunk_of_col = lax.broadcasted_iota(I32, (LANES, LANES), 1) // c
    for d in range(2):
        for tile in range(lc // LANES):
            rows = slice(tile * LANES, (tile + 1) * LANES)
            v_t = v_s[d, rows, :].T.astype(BF16)
            lhs = jnp.concatenate([jnp.where(chunk_of_col == ci, v_t, 0.0) for ci in range(per_tile)], axis=0)
            u_all = jnp.dot(lhs, kt_s[d, rows, :].astype(BF16), preferred_element_type=F32)
            for ci in range(per_tile):
                u_s[d, tile * per_tile + ci] = u_all[ci * LANES:(ci + 1) * LANES]

    def scan_step(ci, states):
        new = []
        for d in range(2):
            cidx = chunk_of(d, ci)
            sb_s[d, cidx] = states[d].astype(BF16)
            r0 = pl.multiple_of(cidx * c, c)
            new.append(states[d] * de_s[d, pl.ds(r0, 1), :] + u_s[d, cidx])
        return tuple(new)

    final = lax.fori_loop(0, n_chunks, scan_step, (st[0], st[1]))
    for d in range(2):
        st[d] = final[d]

    ones = jnp.ones((LANES, LANES), BF16)
    t_in_group = lax.broadcasted_iota(I32, (SUBLANES, LANES), 0)
    groups = c // SUBLANES
    for ci in range(n_chunks):
        rows = slice(ci * c, (ci + 1) * c)
        for d, o_ref in enumerate((of_ref, ob_ref)):
            qs, qt, bl = (ref[d, rows, :] for ref in (qs_s, qt_s, bl_s))
            o = lax.dot_general(qt.astype(BF16), sb_s[d, ci], (((1,), (1,)), ((), ())),
                                preferred_element_type=F32)
            terms = []
            for si in range(c):
                gs = si // SUBLANES
                row = slice(ci * c + si, ci * c + si + 1)
                for gi in (range(gs, groups) if d == 0 else range(0, gs + 1)):
                    sl = slice(gi * SUBLANES, (gi + 1) * SUBLANES)
                    diff = bl[sl] - cl_s[d, row, :]
                    if gi == gs:
                        t = t_in_group + gi * SUBLANES
                        diff = jnp.where((t >= si) if d == 0 else (t <= si), diff, -jnp.inf)
                    terms.append((row, gi, qs[sl] * jnp.exp2(diff)))
            stacked = jnp.concatenate([x for _, _, x in terms], axis=0).astype(BF16)
            score = jnp.dot(stacked, ones, preferred_element_type=F32)
            parts = [o[gi * SUBLANES:(gi + 1) * SUBLANES] for gi in range(groups)]
            for n, (row, gi, _) in enumerate(terms):
                parts[gi] = parts[gi] + score[n * SUBLANES:(n + 1) * SUBLANES] * v_s[d, row, :]
            o_ref[rows, :] = jnp.concatenate(parts, axis=0).astype(o_ref.dtype)

    @pl.when(is_ctx)
    def _():
        for d in range(2):
            sfin_ref[d] = st[d].T


def _hgrn(uh, lb, s0, lay):
    h = lay["heads"]
    lc = lay["lc"]
    ncs = lay["tc"] // lc
    spl = lay["lat_len"] // lc
    nls = lay["tl"] // lc
    nc = lc // SCAN_CHUNK
    assert lay["ctx_len"] == lc

    row = lax.broadcasted_iota(I32, (lc, lc), 0)
    col = lax.broadcasted_iota(I32, (lc, lc), 1)
    same = (row // SCAN_CHUNK) == (col // SCAN_CHUNK)
    tri_f = jnp.logical_and(same, col <= row).astype(BF16)
    tri_b = jnp.logical_and(same, col >= row).astype(BF16)

    def fwd_blk(s):
        return s

    def bwd_blk(s):
        q = s - ncs
        return jnp.where(s < ncs, s, ncs + (q // spl) * spl + (spl - 1 - q % spl))

    def lat_seq(s):
        return jnp.maximum(s - ncs, 0) // spl

    def part(p, blk):
        return pl.BlockSpec((None, lc, LANES), lambda hh, s: (p * h + hh, blk(s), 0))

    const = pl.BlockSpec((lc, lc), lambda hh, s: (0, 0))
    o_spec_f = pl.BlockSpec((None, lc, LANES), lambda hh, s: (hh, fwd_blk(s), 0))
    o_spec_b = pl.BlockSpec((None, lc, LANES), lambda hh, s: (hh, bwd_blk(s), 0))
    state_shape = (lay["ctx_batch"], 2, h, LANES, LANES)
    sfin_spec = pl.BlockSpec((None, 2, None, LANES, LANES), lambda hh, s: (jnp.minimum(s, ncs - 1), 0, hh, 0, 0))
    scr = pltpu.VMEM((2, lc, LANES), F32)
    return pl.pallas_call(
        functools.partial(_hgrn_kernel, n_ctx_steps=ncs, lat_steps_per_seq=spl),
        grid=(h, ncs + nls),
        in_specs=[
            part(0, fwd_blk), part(1, fwd_blk), part(2, fwd_blk),
            part(0, bwd_blk), part(1, bwd_blk), part(3, bwd_blk),
            pl.BlockSpec((2, None, 1, LANES), lambda hh, s: (0, hh, 0, 0)),
            pl.BlockSpec((None, 2, None, LANES, LANES), lambda hh, s: (lat_seq(s), 0, hh, 0, 0)),
            const, const, const,
        ],
        out_specs=[o_spec_f, o_spec_b, sfin_spec],
        out_shape=[
            jax.ShapeDtypeStruct((h, lay["t"], LANES), BF16),
            jax.ShapeDtypeStruct((h, lay["t"], LANES), BF16),
            jax.ShapeDtypeStruct(state_shape, F32),
        ],
        scratch_shapes=[pltpu.VMEM((2, LANES, LANES), F32)] + [scr] * 7 + [
            pltpu.VMEM((2, nc, LANES, LANES), F32), pltpu.VMEM((2, nc, LANES, LANES), BF16)],
        compiler_params=_params("arbitrary", "arbitrary"),
        name="hgrn_scan",
    )(uh, uh, uh, uh, uh, uh, lb, s0, tri_f, tri_b, same.astype(BF16))


def _branches_kernel(*refs, n_groups, n_ctx_blocks, seg_ctx, seg_lat, heads, d_a, d_b, d_model):
    of_ref, ob_ref = refs[:2]
    u_refs = refs[2:2 + n_groups]
    hg_ref, cw_ref, cb_ref, lng_ref, lnb_ref, wa_ref, wb_ref, o_ref, pad_c, pad_l, cv = refs[2 + n_groups:]
    bm = o_ref.shape[0]
    m = pl.program_id(0)
    taps = cw_ref.shape[0]
    half = taps // 2

    def u_tile(col):
        tile = col // LANES
        return u_refs[tile // heads][tile % heads].astype(F32)

    def u_cols(col, width):
        return jnp.concatenate([u_tile(col + i * LANES) for i in range(width // LANES)], axis=1)

    o_h = [of_ref[h].astype(F32) + ob_ref[h].astype(F32) for h in range(heads)]
    ss = o_h[0] * o_h[0]
    for h in range(1, heads):
        ss = ss + o_h[h] * o_h[h]
    inv = lax.rsqrt(jnp.sum(ss, axis=-1, keepdims=True) / d_a + EPS)
    gated = []
    for h in range(heads):
        sl = slice(h * LANES, (h + 1) * LANES)
        gated.append((o_h[h] * inv * hg_ref[:, sl]) * _silu(u_tile(h * LANES)))
    y_a = jnp.dot(jnp.concatenate(gated, axis=1).astype(BF16), wa_ref[...], preferred_element_type=F32)

    glu = u_cols(d_a, d_b) * jax.nn.sigmoid(u_cols(d_a + d_b, d_b))

    n_tiles = d_b // LANES

    def conv(pad, seg):
        stride = seg + 2 * CONV_PAD
        zeros = jnp.zeros((CONV_PAD, n_tiles, LANES), F32)
        for i in range(bm // seg):
            pad[i * stride:i * stride + CONV_PAD] = zeros
            pad[i * stride + CONV_PAD + seg:(i + 1) * stride] = zeros
            for s in range(n_tiles):
                pad[i * stride + CONV_PAD:i * stride + CONV_PAD + seg, s, :] = (
                    glu[i * seg:(i + 1) * seg, s * LANES:(s + 1) * LANES])
        rows = SCAN_CHUNK

        def time_step(ci, carry):
            t0 = ci * rows
            p0 = t0 + (t0 // seg) * 2 * CONV_PAD + CONV_PAD - half
            acc = jnp.zeros((rows, n_tiles, LANES), F32)
            for j in range(taps):
                acc = acc + pad[pl.ds(p0 + j, rows)] * cw_ref[j]
            cv[pl.ds(t0, rows)] = acc
            return carry

        lax.fori_loop(0, bm // rows, time_step, 0)

    @pl.when(m < n_ctx_blocks)
    def _():
        conv(pad_c, seg_ctx)

    @pl.when(m >= n_ctx_blocks)
    def _():
        conv(pad_l, seg_lat)

    v = jnp.concatenate([cv[:, s, :] for s in range(n_tiles)], axis=1) + cb_ref[...]
    mu = jnp.mean(v, axis=-1, keepdims=True)
    var = jnp.mean(jnp.square(v - mu), axis=-1, keepdims=True)
    v = (v - mu) * lax.rsqrt(var + EPS) * lng_ref[...] + lnb_ref[...]
    y_b = jnp.dot(_silu(v).astype(BF16), wb_ref[...], preferred_element_type=F32)

    g0 = d_a + 2 * d_b
    g_a = jax.nn.sigmoid(u_cols(g0, d_model))
    g_b = jax.nn.sigmoid(u_cols(g0 + d_model, d_model))
    o_ref[...] = (g_a * y_a + g_b * y_b).astype(o_ref.dtype)


def _branches(o_f, o_b, u, hg, cw, cb, lng, lnb, wa, wb, lay):
    heads, d_a, d_b, d = lay["heads"], lay["d_a"], lay["d_b"], lay["d"]
    bm = lay["bm3"]
    ncb = lay["tc"] // bm
    seg_c = min(lay["ctx_len"], bm)
    seg_l = GRID_W
    taps = cw.shape[0]
    const = lambda m: (0, 0)
    resident = dict(pipeline_mode=pl.Buffered(1))
    assert d_b % d_a == 0 and d % d_a == 0
    n_groups = (d_a + 2 * d_b + 2 * d) // d_a

    def pad_rows(seg):
        return (bm // seg) * (seg + 2 * CONV_PAD)

    def group(i):
        return pl.BlockSpec((heads, bm, LANES), lambda m: (4 + i, m, 0))

    return pl.pallas_call(
        functools.partial(_branches_kernel, n_groups=n_groups, n_ctx_blocks=ncb, seg_ctx=seg_c, seg_lat=seg_l,
                          heads=heads, d_a=d_a, d_b=d_b, d_model=d),
        grid=(lay["t"] // bm,),
        in_specs=[
            pl.BlockSpec((heads, bm, LANES), lambda m: (0, m, 0)),
            pl.BlockSpec((heads, bm, LANES), lambda m: (0, m, 0)),
            *[group(i) for i in range(n_groups)],
            pl.BlockSpec((1, d_a), const),
            pl.BlockSpec((taps, d_b // LANES, LANES), lambda m: (0, 0, 0)),
            pl.BlockSpec((1, d_b), const),
            pl.BlockSpec((1, d_b), const),
            pl.BlockSpec((1, d_b), const),
            pl.BlockSpec((d_a, d), const, **resident),
            pl.BlockSpec((d_b, d), const, **resident),
        ],
        out_specs=pl.BlockSpec((bm, d), lambda m: (m, 0)),
        out_shape=jax.ShapeDtypeStruct((lay["t"], d), BF16),
        scratch_shapes=[
            pltpu.VMEM((pad_rows(seg_c), d_b // LANES, LANES), F32),
            pltpu.VMEM((pad_rows(seg_l), d_b // LANES, LANES), F32),
            pltpu.VMEM((bm, d_b // LANES, LANES), F32),
        ],
        compiler_params=_params("parallel"),
        name="branches_merge",
    )(o_f, o_b, *([u] * n_groups), hg, cw, cb, lng, lnb, wa, wb)


def _pack_bf16_pairs(lo, hi):
    lo_bits = lax.bitcast_convert_type(lo, U32)
    hi_bits = lax.bitcast_convert_type(hi, U32)
    return (lo_bits >> 16) | (hi_bits & jnp.uint32(0xFFFF0000))


def _unpack_bf16_pairs(packed):
    lo = lax.bitcast_convert_type(packed << 16, F32)
    hi = lax.bitcast_convert_type(packed & jnp.uint32(0xFFFF0000), F32)
    return lo, hi


def _outproj_kernel(mix_ref, xp_ref, xs_ref, g1_ref, ng_ref, sh2_ref, sc2_ref, wo_ref, wr_ref, br_ref,
                    x1_ref, hp_ref, rt_ref, cnt_ref, *, n_ctx_blocks, n_experts):
    m = pl.program_id(0)

    @pl.when(m == 0)
    def _():
        cnt_ref[...] = jnp.zeros(cnt_ref.shape, F32)

    x = jnp.where(m < n_ctx_blocks, xp_ref[...], xs_ref[...])
    y = jnp.dot(mix_ref[...], wo_ref[...], preferred_element_type=F32)
    x1 = x + g1_ref[...] * y
    x1_ref[...] = x1
    hb = _mod_rmsnorm(x1, ng_ref[...], sc2_ref[...], sh2_ref[...]).astype(BF16)
    hf = hb.astype(F32)
    half = hf.shape[1] // 2
    hp_ref[...] = _pack_bf16_pairs(hf[:, :half], hf[:, half:])

    logits = jnp.dot(hb, wr_ref[...], preferred_element_type=F32) + br_ref[...]
    lane = lax.broadcasted_iota(I32, logits.shape, 1).astype(F32)
    logits = jnp.where(lane < n_experts, logits, -jnp.inf)
    vals, idxs = [], []
    for _ in range(TOP_K):
        mx = jnp.max(logits, axis=-1, keepdims=True)
        ix = jnp.min(jnp.where(logits == mx, lane, float(LANES)), axis=-1, keepdims=True)
        vals.append(mx)
        idxs.append(ix)
        logits = jnp.where(lane == ix, -jnp.inf, logits)
    es = [jnp.exp(v - vals[0]) for v in vals]
    tot = es[0]
    for e in es[1:]:
        tot = tot + e
    hot = [(lane == ix).astype(F32) for ix in idxs]
    hot_all = hot[0] + hot[1] + hot[2] + hot[3]
    bm = hot_all.shape[0]
    earlier = (lax.broadcasted_iota(I32, (bm, bm), 1) < lax.broadcasted_iota(I32, (bm, bm), 0)).astype(BF16)
    before = jnp.dot(earlier, hot_all.astype(BF16), preferred_element_type=F32) + cnt_ref[...]
    out = jnp.zeros(logits.shape, F32)
    for k in range(TOP_K):
        out = jnp.where(lane == k, idxs[k], out)
        out = jnp.where(lane == TOP_K + k, es[k] / tot, out)
        out = jnp.where(lane == 2 * TOP_K + k, jnp.sum(before * hot[k], axis=-1, keepdims=True), out)
    rt_ref[...] = out
    cnt_ref[...] = before[bm - 1:bm, :] + hot_all[bm - 1:bm, :]


def _outproj(mix, xp, xs, mods, ng, wo, wr, br, lay):
    d = lay["d"]
    bm = lay["bm3"]
    ncb = lay["tc"] // bm
    t = lay["t"]

    def seq_row(m):
        return jnp.where(m < ncb, lay["ctx_row"], ((m - ncb) * bm) // lay["lat_len"])

    const = lambda m: (0, 0)
    mod = lambda part: pl.BlockSpec((None, 1, d), lambda m: (seq_row(m) * 6 + part, 0, 0))
    return pl.pallas_call(
        functools.partial(_outproj_kernel, n_ctx_blocks=ncb, n_experts=lay["experts"]),
        grid=(t // bm,),
        in_specs=[
            pl.BlockSpec((bm, d), lambda m: (m, 0)),
            pl.BlockSpec((bm, d), lambda m: (jnp.minimum(m, ncb - 1), 0)),
            pl.BlockSpec((bm, d), lambda m: (jnp.maximum(m - ncb, 0), 0)),
            mod(2),
            pl.BlockSpec((1, d), const),
            mod(3),
            mod(4),
            pl.BlockSpec((d, d), const, pipeline_mode=pl.Buffered(1)),
            pl.BlockSpec((d, LANES), const),
            pl.BlockSpec((1, LANES), const),
        ],
        out_specs=[
            pl.BlockSpec((bm, d), lambda m: (m, 0)),
            pl.BlockSpec((bm, d // 2), lambda m: (m, 0)),
            pl.BlockSpec((bm, LANES), lambda m: (m, 0)),
            pl.BlockSpec((1, LANES), const),
        ],
        out_shape=[
            jax.ShapeDtypeStruct((t, d), F32),
            jax.ShapeDtypeStruct((t, d // 2), U32),
            jax.ShapeDtypeStruct((t, LANES), F32),
            jax.ShapeDtypeStruct((1, LANES), F32),
        ],
        compiler_params=_params("arbitrary"),
        name="outproj_router",
    )(mix, xp, xs, mods, ng, mods, mods, wo, wr, br)


def _routing_tables(rt, counts_row, lay):
    e, bm = lay["experts"], lay["bm_moe"]
    nb = lay["moe_blocks"]
    idx = rt[:, :TOP_K].astype(I32)
    rank = rt[:, 2 * TOP_K:3 * TOP_K].astype(I32)
    counts = counts_row[0, :e].astype(I32)
    nblk = (counts + bm - 1) // bm
    blk_end = jnp.cumsum(nblk).astype(I32)
    blk_start = blk_end - nblk
    experts = jnp.arange(e, dtype=I32)
    first_slot = jnp.sum(jnp.where(idx[:, :, None] == experts, blk_start * bm, 0), axis=-1)
    pos = first_slot + rank
    n_real = blk_end[-1]

    def lookup(table, i):
        return jnp.sum(jnp.where(i[:, None] == experts, table, 0), axis=-1)

    def schedule(nt):
        s = jnp.arange(nb * nt, dtype=I32)
        ex = jnp.minimum(jnp.sum((blk_end * nt <= s[:, None]).astype(I32), axis=-1), e - 1)
        r = s - lookup(blk_start, ex) * nt
        per = jnp.maximum(lookup(nblk, ex), 1)
        real = s < n_real * nt
        tail = s - n_real * nt
        m_idx = jnp.where(real, lookup(blk_start, ex) + r % per, n_real + tail // nt)
        n_out = jnp.where(real, r // per, tail % nt)
        is_last = s == jnp.maximum(n_real * nt - 1, 0)
        e_last = jnp.sum(jnp.where(is_last, ex, 0))
        n_last = jnp.sum(jnp.where(is_last, r // per, 0))
        e_w = jnp.where(real, ex, e_last)
        n_w = jnp.where(real, n_out, n_last)
        flag = jnp.where(real, 1 + 2 * (r % per == 0).astype(I32), 0)
        return m_idx.astype(I32), n_out.astype(I32), e_w.astype(I32), n_w.astype(I32), flag.astype(I32)

    return pos, schedule(lay["nt_up"]), schedule(lay["nt_down"])


def _dispatch_kernel(pos_ref, h_ref, zero_ref, out_ref, sem):
    del zero_ref
    bt = h_ref.shape[0] * SUBLANES

    def issue(i, carry):
        for u in range(SUBLANES):
            for k in range(TOP_K):
                slot = pos_ref[0, k * bt + i * SUBLANES + u]
                pltpu.make_async_copy(h_ref.at[i, pl.ds(u, 1), :], out_ref.at[pl.ds(slot, 1), :], sem).start()
        return carry

    lax.fori_loop(0, bt // SUBLANES, issue, 0)
    n = TOP_K * bt
    pltpu.make_async_copy(out_ref.at[pl.ds(0, n), :], out_ref.at[pl.ds(0, n), :], sem).wait()


def _dispatch_rows(hp, pos_blocks, lay):
    bt = lay["bt"]
    t, width = hp.shape
    rows = lay["moe_blocks"] * lay["bm_moe"]
    return pl.pallas_call(
        _dispatch_kernel,
        grid=(t // bt,),
        in_specs=[
            pl.BlockSpec((None, 1, TOP_K * bt), lambda m: (m, 0, 0), memory_space=pltpu.SMEM),
            pl.BlockSpec((bt // SUBLANES, SUBLANES, width), lambda m: (m, 0, 0)),
            pl.BlockSpec(memory_space=pl.ANY),
        ],
        out_specs=pl.BlockSpec(memory_space=pl.ANY),
        out_shape=jax.ShapeDtypeStruct((rows, width), hp.dtype),
        scratch_shapes=[pltpu.SemaphoreType.DMA(())],
        input_output_aliases={2: 0},
        compiler_params=_params("arbitrary"),
        name="moe_dispatch",
    )(pos_blocks, hp.reshape(t // SUBLANES, SUBLANES, width), jnp.zeros((rows, width), hp.dtype))


def _up_kernel(sm, sn, se, sw, sf, x_ref, wg_ref, wu_ref, bg_ref, bu_ref, o_ref, wg_bf, wu_bf):
    flag = sf[pl.program_id(0)]

    @pl.when(flag >= 2)
    def _():
        wg_bf[...] = wg_ref[...].astype(BF16)
        wu_bf[...] = wu_ref[...].astype(BF16)

    @pl.when(flag >= 1)
    def _():
        lo, hi = _unpack_bf16_pairs(x_ref[...])
        x = jnp.concatenate([lo.astype(BF16), hi.astype(BF16)], axis=1)
        gate = jnp.dot(x, wg_bf[...], preferred_element_type=F32) + bg_ref[...]
        up = jnp.dot(x, wu_bf[...], preferred_element_type=F32) + bu_ref[...]
        gate = jnp.minimum(gate, SWIGLU_LIMIT)
        up = jnp.clip(up, -SWIGLU_LIMIT, SWIGLU_LIMIT)
        o_ref[...] = ((up + 1.0) * gate * jax.nn.sigmoid(SWIGLU_ALPHA * gate)).astype(o_ref.dtype)

    @pl.when(flag == 0)
    def _():
        o_ref[...] = jnp.zeros(o_ref.shape, o_ref.dtype)


def _moe_up(xs, w_gu, b_gu, sched, lay):
    bm, nb, nt = lay["bm_moe"], lay["moe_blocks"], lay["nt_up"]
    d, ff = lay["d"], lay["d_ff"]
    tn = ff // nt
    e = lay["experts"]
    grid_spec = pltpu.PrefetchScalarGridSpec(
        num_scalar_prefetch=5,
        grid=(nb * nt,),
        in_specs=[
            pl.BlockSpec((bm, d // 2), lambda s, sm, sn, se, sw, sf: (sm[s], 0)),
            pl.BlockSpec((None, d, tn), lambda s, sm, sn, se, sw, sf: (se[s], 0, sw[s])),
            pl.BlockSpec((None, d, tn), lambda s, sm, sn, se, sw, sf: (se[s], 0, nt + sw[s])),
            pl.BlockSpec((None, 1, tn), lambda s, sm, sn, se, sw, sf: (se[s], 0, sw[s])),
            pl.BlockSpec((None, 1, tn), lambda s, sm, sn, se, sw, sf: (se[s], 0, nt + sw[s])),
        ],
        out_specs=pl.BlockSpec((bm, tn), lambda s, sm, sn, se, sw, sf: (sm[s], sn[s])),
        scratch_shapes=[pltpu.VMEM((d, tn), BF16), pltpu.VMEM((d, tn), BF16)],
    )
    return pl.pallas_call(
        _up_kernel,
        grid_spec=grid_spec,
        out_shape=jax.ShapeDtypeStruct((nb * bm, ff), BF16),
        compiler_params=_params("arbitrary"),
        name="moe_gate_up",
    )(*sched, xs, w_gu, w_gu, b_gu.reshape(e, 1, 2 * ff), b_gu.reshape(e, 1, 2 * ff))


def _down_kernel(sm, sn, se, sw, sf, a_ref, wl_ref, wh_ref, bl_ref, bh_ref, o_ref, wl_bf, wh_bf):
    flag = sf[pl.program_id(0)]

    @pl.when(flag >= 2)
    def _():
        wl_bf[...] = wl_ref[...].astype(BF16)
        wh_bf[...] = wh_ref[...].astype(BF16)

    @pl.when(flag >= 1)
    def _():
        a = a_ref[...]
        lo = jnp.dot(a, wl_bf[...], preferred_element_type=F32) + bl_ref[...]
        hi = jnp.dot(a, wh_bf[...], preferred_element_type=F32) + bh_ref[...]
        o_ref[...] = _pack_bf16_pairs(lo.astype(BF16).astype(F32), hi.astype(BF16).astype(F32))

    @pl.when(flag == 0)
    def _():
        o_ref[...] = jnp.zeros(o_ref.shape, o_ref.dtype)


def _moe_down(act, w_dn, b_dn, sched, lay):
    bm, nb, nt = lay["bm_moe"], lay["moe_blocks"], lay["nt_down"]
    d, ff = lay["d"], lay["d_ff"]
    half = d // 2
    tn = half // nt
    e = lay["experts"]
    grid_spec = pltpu.PrefetchScalarGridSpec(
        num_scalar_prefetch=5,
        grid=(nb * nt,),
        in_specs=[
            pl.BlockSpec((bm, ff), lambda s, sm, sn, se, sw, sf: (sm[s], 0)),
            pl.BlockSpec((None, ff, tn), lambda s, sm, sn, se, sw, sf: (se[s], 0, sw[s])),
            pl.BlockSpec((None, ff, tn), lambda s, sm, sn, se, sw, sf: (se[s], 0, nt + sw[s])),
            pl.BlockSpec((None, 1, tn), lambda s, sm, sn, se, sw, sf: (se[s], 0, sw[s])),
            pl.BlockSpec((None, 1, tn), lambda s, sm, sn, se, sw, sf: (se[s], 0, nt + sw[s])),
        ],
        out_specs=pl.BlockSpec((bm, tn), lambda s, sm, sn, se, sw, sf: (sm[s], sn[s])),
        scratch_shapes=[pltpu.VMEM((ff, tn), BF16), pltpu.VMEM((ff, tn), BF16)],
    )
    return pl.pallas_call(
        _down_kernel,
        grid_spec=grid_spec,
        out_shape=jax.ShapeDtypeStruct((nb * bm, half), U32),
        compiler_params=_params("arbitrary"),
        name="moe_down",
    )(*sched, act, w_dn, w_dn, b_dn.reshape(e, 1, d), b_dn.reshape(e, 1, d))


def _combine_kernel(pos_ref, y_ref, rt_ref, x1_ref, g2_ref, fg_ref, o_ref, buf, sem):
    bt = o_ref.shape[0]
    n = pos_ref.shape[1]

    def issue(i, carry):
        for u in range(SUBLANES):
            row = pos_ref[0, i * SUBLANES + u]
            pltpu.make_async_copy(y_ref.at[pl.ds(row, 1), :], buf.at[i, pl.ds(u, 1), :], sem).start()
        return carry

    lax.fori_loop(0, n // SUBLANES, issue, 0)
    pltpu.make_async_copy(buf, buf, sem).wait()
    lo = hi = None
    g = bt // SUBLANES
    for k in range(TOP_K):
        p = rt_ref[:, TOP_K + k:TOP_K + k + 1]
        lo_k, hi_k = _unpack_bf16_pairs(buf[k * g:(k + 1) * g].reshape(bt, buf.shape[2]))
        lo = lo_k * p if lo is None else lo + lo_k * p
        hi = hi_k * p if hi is None else hi + hi_k * p
    x2 = x1_ref[...] + g2_ref[...] * jnp.concatenate([lo, hi], axis=1)
    o_ref[...] = x2 * lax.rsqrt(jnp.mean(x2 * x2, axis=-1, keepdims=True) + EPS) * fg_ref[...]


def _combine(y_sorted, pos, rt, x1, mods, fg, lay, *, blk0, n_tok, seq_len, row0):
    d = lay["d"]
    bt = lay["bt"]
    nblk = n_tok // bt

    def seq_row(m):
        return row0 if seq_len is None else row0 + (m * bt) // seq_len

    return pl.pallas_call(
        _combine_kernel,
        grid=(nblk,),
        in_specs=[
            pl.BlockSpec((None, 1, TOP_K * bt), lambda m: (blk0 + m, 0, 0), memory_space=pltpu.SMEM),
            pl.BlockSpec(memory_space=pl.ANY),
            pl.BlockSpec((bt, LANES), lambda m: (blk0 + m, 0)),
            pl.BlockSpec((bt, d), lambda m: (blk0 + m, 0)),
            pl.BlockSpec((None, 1, d), lambda m: (seq_row(m) * 6 + 5, 0, 0)),
            pl.BlockSpec((1, d), lambda m: (0, 0)),
        ],
        out_specs=pl.BlockSpec((bt, d), lambda m: (m, 0)),
        out_shape=jax.ShapeDtypeStruct((n_tok, d), F32),
        scratch_shapes=[pltpu.VMEM((TOP_K * bt // SUBLANES, SUBLANES, d // 2), U32), pltpu.SemaphoreType.DMA(())],
        compiler_params=_params("arbitrary"),
        name="moe_combine_final",
    )(pos, y_sorted, rt, x1, mods, fg)


def _layout(x_prompt, x_sample, state_hgrn, w_gate_up):
    cb, cl, d = x_prompt.shape
    lb_, ll, _ = x_sample.shape
    heads = state_hgrn.shape[3]
    experts, _, ff2 = w_gate_up.shape[1:]
    tc, tl = cb * cl, lb_ * ll
    t = tc + tl
    bm_moe = 512 if t * TOP_K >= 32768 else 128
    lay = dict(
        d=d, t=t, tc=tc, tl=tl, ctx_batch=cb, ctx_len=cl, lat_batch=lb_, lat_len=ll, ctx_row=lb_,
        heads=heads, d_a=heads * LANES, experts=experts, d_ff=ff2 // 2,
        bm1=512 if (tc % 512 == 0 and ll % 512 == 0) else cl, bn1=min(1024, heads * LANES), lc=cl, bm3=cl,
        bt=min(256, cl),
        bm_moe=bm_moe, moe_blocks=pl.cdiv(t * TOP_K, bm_moe) + experts,
        nt_up=max(1, (ff2 // 2) // 512), nt_down=max(1, (d // 2) // 512),
    )
    assert state_hgrn.shape[4] == LANES and state_hgrn.shape[5] == LANES
    assert tc % lay["bm1"] == 0 and tl % lay["bm1"] == 0 and ll % lay["bm1"] == 0
    assert ll % cl == 0 and cl % GRID_W == 0 and cl % SCAN_CHUNK == 0
    return lay


def kernel(x_prompt, x_sample, state_hgrn, c, c_ctx, w_ada, b_ada, norm_mix_g, w_in, lb_logits, hgrn_norm_g, w_a_out, conv_w, conv_b, conv_ln_g, conv_ln_b, w_b_out, w_out, norm_moe_g, w_router, b_router, w_gate_up, b_gate_up, w_down, b_down, final_norm_g):
    assert w_ada.shape[0] == 1, "one trunk layer"
    lay = _layout(x_prompt, x_sample, state_hgrn, w_gate_up)
    d, d_a, heads = lay["d"], lay["d_a"], lay["heads"]
    d_b = conv_w.shape[2]
    lay["d_b"] = d_b
    xp = x_prompt.reshape(lay["tc"], d)
    xs = x_sample.reshape(lay["tl"], d)

    n_rows = lay["lat_batch"] + 1
    rows_pad = pl.cdiv(n_rows, SUBLANES) * SUBLANES
    cond = jnp.concatenate([c, c_ctx[None, :], jnp.zeros((rows_pad - n_rows, d), F32)], axis=0)
    mods = _adaln(cond, w_ada.reshape(d, 6 * d), b_ada.reshape(1, 6 * d)).reshape(rows_pad * 6, 1, d)

    w_in_bf = w_in.reshape(d, -1).astype(BF16)
    u = _inproj(xp, xs, norm_mix_g.reshape(1, d), mods, w_in_bf, lay)

    lb_all = jnp.cumsum(jax.nn.softmax(lb_logits.astype(F32), axis=0), axis=0)[0]
    s0 = state_hgrn.reshape(lay["lat_batch"], 2, heads, LANES, LANES)
    o_f, o_b, s_fin = _hgrn(u, lb_all.reshape(2, heads, 1, LANES), s0, lay)

    mix = _branches(o_f, o_b, u, hgrn_norm_g.reshape(1, d_a), conv_w.reshape(-1, d_b // LANES, LANES), conv_b.reshape(1, d_b),
                    conv_ln_g.reshape(1, d_b), conv_ln_b.reshape(1, d_b), w_a_out.reshape(d_a, d).astype(BF16),
                    w_b_out.reshape(d_b, d).astype(BF16), lay)

    e, ff = lay["experts"], lay["d_ff"]
    wr = jnp.zeros((d, LANES), BF16).at[:, :e].set(w_router.reshape(d, e).astype(BF16))
    br = jnp.zeros((1, LANES), F32).at[:, :e].set(b_router.reshape(1, e))
    x1, hp, rt, counts = _outproj(mix, xp, xs, mods, norm_moe_g.reshape(1, d), w_out.reshape(d, d).astype(BF16), wr,
                                  br, lay)

    pos, sched_up, sched_down = _routing_tables(rt, counts, lay)
    bt = lay["bt"]
    pos_blocks = pos.reshape(lay["t"] // bt, bt, TOP_K).transpose(0, 2, 1).reshape(lay["t"] // bt, 1, TOP_K * bt)
    x_sorted = _dispatch_rows(hp, pos_blocks, lay)
    act = _moe_up(x_sorted, w_gate_up.reshape(e, d, 2 * ff), b_gate_up.reshape(e, 2 * ff), sched_up, lay)
    y_sorted = _moe_down(act, w_down.reshape(e, ff, d), b_down.reshape(e, d), sched_down, lay)

    fg = final_norm_g.reshape(1, d)
    y_prompt = _combine(y_sorted, pos_blocks, rt, x1, mods, fg, lay, blk0=0, n_tok=lay["tc"],
                        seq_len=None, row0=lay["ctx_row"])
    y_sample = _combine(y_sorted, pos_blocks, rt, x1, mods, fg, lay, blk0=lay["tc"] // bt, n_tok=lay["tl"],
                        seq_len=lay["lat_len"], row0=0)
    return (y_prompt.reshape(x_prompt.shape), y_sample.reshape(x_sample.shape),
            s_fin.reshape(lay["ctx_batch"], 1, 2, heads, LANES, LANES))
```

```python
import functools

import jax
import jax.numpy as jnp
from jax import lax
from jax.experimental import pallas as pl
from jax.experimental.pallas import tpu as pltpu

F32 = jnp.float32
BF16 = jnp.bfloat16
U32 = jnp.uint32
I32 = jnp.int32

EPS = 1e-6
TOP_K = 4
GRID_W = 64
SWIGLU_LIMIT = 7.0
SWIGLU_ALPHA = 1.702

LANES = 128
SUBLANES = 8
SCAN_CHUNK = 16
CONV_PAD = 16
V7X_VMEM_LIMIT = 56 * 2**20


def _params(*sem):
    return pltpu.CompilerParams(dimension_semantics=sem, vmem_limit_bytes=V7X_VMEM_LIMIT)


def _tile(n, target):
    t = min(n, target) // LANES * LANES
    while n % t:
        t -= LANES
    return t


def _silu(x):
    return x * jax.nn.sigmoid(x)


def _mod_rmsnorm(x, g, scale, shift):
    y = x * lax.rsqrt(jnp.mean(x * x, axis=-1, keepdims=True) + EPS) * g
    return y * (1.0 + scale) + shift


def _ada_kernel(c_ref, w_ref, b_ref, o_ref):
    s = _silu(c_ref[...]).astype(BF16)
    o_ref[...] = jnp.dot(s, w_ref[...].astype(BF16), preferred_element_type=F32) + b_ref[...]


def _adaln(cond, w, b):
    rows, d = cond.shape
    n = w.shape[1]
    tn = _tile(n, 1024)
    return pl.pallas_call(
        _ada_kernel,
        grid=(n // tn,),
        in_specs=[
            pl.BlockSpec((rows, d), lambda j: (0, 0)),
            pl.BlockSpec((d, tn), lambda j: (0, j)),
            pl.BlockSpec((1, tn), lambda j: (0, j)),
        ],
        out_specs=pl.BlockSpec((rows, tn), lambda j: (0, j)),
        out_shape=jax.ShapeDtypeStruct((rows, n), F32),
        compiler_params=_params("arbitrary"),
        name="adaln",
    )(cond, w, b)


def _inproj_kernel(xp_ref, xs_ref, g_ref, sh_ref, sc_ref, w_ref, o_ref, h_scr, *, n_ctx_blocks):
    m = pl.program_id(0)

    @pl.when(pl.program_id(1) == 0)
    def _():
        x = jnp.where(m < n_ctx_blocks, xp_ref[...], xs_ref[...])
        h_scr[...] = _mod_rmsnorm(x, g_ref[...], sc_ref[...], sh_ref[...]).astype(BF16)

    acc = jnp.dot(h_scr[...], w_ref[...], preferred_element_type=F32)
    for j in range(o_ref.shape[0]):
        o_ref[j] = acc[:, j * LANES:(j + 1) * LANES].astype(o_ref.dtype)


def _inproj(xp, xs, g, mods, w, lay):
    d = xp.shape[1]
    n = w.shape[1]
    bm, bn = lay["bm1"], _tile(n, lay["bn1"])
    ncb = lay["tc"] // bm
    nblk = lay["t"] // bm

    def seq_row(m):
        return jnp.where(m < ncb, lay["ctx_row"], ((m - ncb) * bm) // lay["lat_len"])

    out_shape = jax.ShapeDtypeStruct((n // LANES, lay["t"], LANES), BF16)
    out_spec = pl.BlockSpec((bn // LANES, bm, LANES), lambda m, j: (j, m, 0))
    return pl.pallas_call(
        functools.partial(_inproj_kernel, n_ctx_blocks=ncb),
        grid=(nblk, n // bn),
        in_specs=[
            pl.BlockSpec((bm, d), lambda m, j: (jnp.minimum(m, ncb - 1), 0)),
            pl.BlockSpec((bm, d), lambda m, j: (jnp.maximum(m - ncb, 0), 0)),
            pl.BlockSpec((1, d), lambda m, j: (0, 0)),
            pl.BlockSpec((None, 1, d), lambda m, j: (seq_row(m) * 6 + 0, 0, 0)),
            pl.BlockSpec((None, 1, d), lambda m, j: (seq_row(m) * 6 + 1, 0, 0)),
            pl.BlockSpec((d, bn), lambda m, j: (0, j)),
        ],
        out_specs=out_spec,
        out_shape=out_shape,
        scratch_shapes=[pltpu.VMEM((bm, d), BF16)],
        compiler_params=_params("parallel", "arbitrary"),
        name="inproj",
    )(xp, xs, g, mods, mods, w)


def _hgrn_kernel(qf_ref, vf_ref, ff_ref, qb_ref, vb_ref, fb_ref, lb_ref, s0_ref, trf_ref, trb_ref, blk_ref,
                 of_ref, ob_ref, sfin_ref, st, qs_s, qt_s, kt_s, bl_s, cl_s, v_s, de_s, u_s, sb_s,
                 *, n_ctx_steps, lat_steps_per_seq):
    c = SCAN_CHUNK
    lc = qf_ref.shape[0]
    n_chunks = lc // c
    s = pl.program_id(1)
    is_ctx = s < n_ctx_steps
    j = jnp.where(is_ctx, 0, (s - n_ctx_steps) % lat_steps_per_seq)
    first = jnp.logical_or(is_ctx, j == 0)

    @pl.when(first)
    def _():
        for d in range(2):
            st[d] = jnp.where(is_ctx, 0.0, s0_ref[d].T)

    def split3_dot(a, g):
        g1 = g.astype(BF16)
        r1 = g - g1.astype(F32)
        g2 = r1.astype(BF16)
        g3 = (r1 - g2.astype(F32)).astype(BF16)
        dot = functools.partial(jnp.dot, preferred_element_type=F32)
        return dot(a, g1) + dot(a, g2) + dot(a, g3)

    log2e = 1.4426950408889634
    for d, (q_ref, v_ref, f_ref, tri_ref) in enumerate(((qf_ref, vf_ref, ff_ref, trf_ref),
                                                        (qb_ref, vb_ref, fb_ref, trb_ref))):
        qs = _silu(q_ref[...].astype(F32))
        lb = lb_ref[d]
        f = lb + (1.0 - lb) * jax.nn.sigmoid(f_ref[...].astype(F32))
        k = 1.0 - f
        g = jnp.log(f)
        b = split3_dot(tri_ref[...], g)
        b_end = split3_dot(blk_ref[...], g)
        qs_s[d] = qs
        qt_s[d] = qs * jnp.exp(b)
        kt_s[d] = k * jnp.exp(b_end - b)
        bl = b * log2e
        bl_s[d] = bl
        cl_s[d] = bl - jnp.log2(k)
        v_s[d] = v_ref[...].astype(F32)
        de_s[d] = jnp.exp(b_end)

    def chunk_of(d, ci):
        return ci if d == 0 else n_chunks - 1 - ci

    per_tile = LANES // c
    chunk_of_col = lax.broadcasted_iota(I32, (LANES, LANES), 1) // c
    for d in range(2):
        for tile in range(lc // LANES):
            rows = slice(tile * LANES, (tile + 1) * LANES)
            v_t = v_s[d, rows, :].T.astype(BF16)
            lhs = jnp.concatenate([jnp.where(chunk_of_col == ci, v_t, 0.0) for ci in range(per_tile)], axis=0)
            u_all = jnp.dot(lhs, kt_s[d, rows, :].astype(BF16), preferred_element_type=F32)
            for ci in range(per_tile):
                u_s[d, tile * per_tile + ci] = u_all[ci * LANES:(ci + 1) * LANES]

    def scan_step(ci, states):
        new = []
        for d in range(2):
            cidx = chunk_of(d, ci)
            sb_s[d, cidx] = states[d].astype(BF16)
            r0 = pl.multiple_of(cidx * c, c)
            new.append(states[d] * de_s[d, pl.ds(r0, 1), :] + u_s[d, cidx])
        return tuple(new)

    final = lax.fori_loop(0, n_chunks, scan_step, (st[0], st[1]))
    for d in range(2):
        st[d] = final[d]

    ones = jnp.ones((LANES, LANES), BF16)
    t_in_group = lax.broadcasted_iota(I32, (SUBLANES, LANES), 0)
    groups = c // SUBLANES
    for ci in range(n_chunks):
        rows = slice(ci * c, (ci + 1) * c)
        for d, o_ref in enumerate((of_ref, ob_ref)):
            qs, qt, bl = (ref[d, rows, :] for ref in (qs_s, qt_s, bl_s))
            o = lax.dot_general(qt.astype(BF16), sb_s[d, ci], (((1,), (1,)), ((), ())),
                                preferred_element_type=F32)
            terms = []
            for si in range(c):
                gs = si // SUBLANES
                row = slice(ci * c + si, ci * c + si + 1)
                for gi in (range(gs, groups) if d == 0 else range(0, gs + 1)):
                    sl = slice(gi * SUBLANES, (gi + 1) * SUBLANES)
                    diff = bl[sl] - cl_s[d, row, :]
                    if gi == gs:
                        t = t_in_group + gi * SUBLANES
                        diff = jnp.where((t >= si) if d == 0 else (t <= si), diff, -jnp.inf)
                    terms.append((row, gi, qs[sl] * jnp.exp2(diff)))
            stacked = jnp.concatenate([x for _, _, x in terms], axis=0).astype(BF16)
            score = jnp.dot(stacked, ones, preferred_element_type=F32)
            parts = [o[gi * SUBLANES:(gi + 1) * SUBLANES] for gi in range(groups)]
            for n, (row, gi, _) in enumerate(terms):
                parts[gi] = parts[gi] + score[n * SUBLANES:(n + 1) * SUBLANES] * v_s[d, row, :]
            o_ref[rows, :] = jnp.concatenate(parts, axis=0).astype(o_ref.dtype)

    @pl.when(is_ctx)
    def _():
        for d in range(2):
            sfin_ref[d] = st[d].T


def _hgrn(uh, lb, s0, lay):
    h = lay["heads"]
    lc = lay["lc"]
    ncs = lay["tc"] // lc
    spl = lay["lat_len"] // lc
    nls = lay["tl"] // lc
    nc = lc // SCAN_CHUNK
    assert lay["ctx_len"] == lc

    row = lax.broadcasted_iota(I32, (lc, lc), 0)
    col = lax.broadcasted_iota(I32, (lc, lc), 1)
    same = (row // SCAN_CHUNK) == (col // SCAN_CHUNK)
    tri_f = jnp.logical_and(same, col <= row).astype(BF16)
    tri_b = jnp.logical_and(same, col >= row).astype(BF16)

    def fwd_blk(s):
        return s

    def bwd_blk(s):
        q = s - ncs
        return jnp.where(s < ncs, s, ncs + (q // spl) * spl + (spl - 1 - q % spl))

    def lat_seq(s):
        return jnp.maximum(s - ncs, 0) // spl

    def part(p, blk):
        return pl.BlockSpec((None, lc, LANES), lambda hh, s: (p * h + hh, blk(s), 0))

    const = pl.BlockSpec((lc, lc), lambda hh, s: (0, 0))
    o_spec_f = pl.BlockSpec((None, lc, LANES), lambda hh, s: (hh, fwd_blk(s), 0))
    o_spec_b = pl.BlockSpec((None, lc, LANES), lambda hh, s: (hh, bwd_blk(s), 0))
    state_shape = (lay["ctx_batch"], 2, h, LANES, LANES)
    sfin_spec = pl.BlockSpec((None, 2, None, LANES, LANES), lambda hh, s: (jnp.minimum(s, ncs - 1), 0, hh, 0, 0))
    scr = pltpu.VMEM((2, lc, LANES), F32)
    return pl.pallas_call(
        functools.partial(_hgrn_kernel, n_ctx_steps=ncs, lat_steps_per_seq=spl),
        grid=(h, ncs + nls),
        in_specs=[
            part(0, fwd_blk), part(1, fwd_blk), part(2, fwd_blk),
            part(0, bwd_blk), part(1, bwd_blk), part(3, bwd_blk),
            pl.BlockSpec((2, None, 1, LANES), lambda hh, s: (0, hh, 0, 0)),
            pl.BlockSpec((None, 2, None, LANES, LANES), lambda hh, s: (lat_seq(s), 0, hh, 0, 0)),
            const, const, const,
        ],
        out_specs=[o_spec_f, o_spec_b, sfin_spec],
        out_shape=[
            jax.ShapeDtypeStruct((h, lay["t"], LANES), BF16),
            jax.ShapeDtypeStruct((h, lay["t"], LANES), BF16),
            jax.ShapeDtypeStruct(state_shape, F32),
        ],
        scratch_shapes=[pltpu.VMEM((2, LANES, LANES), F32)] + [scr] * 7 + [
            pltpu.VMEM((2, nc, LANES, LANES), F32), pltpu.VMEM((2, nc, LANES, LANES), BF16)],
        compiler_params=_params("arbitrary", "arbitrary"),
        name="hgrn_scan",
    )(uh, uh, uh, uh, uh, uh, lb, s0, tri_f, tri_b, same.astype(BF16))


def _branches_kernel(*refs, n_groups, n_ctx_blocks, seg_ctx, seg_lat, heads, d_a, d_b, d_model):
    of_ref, ob_ref = refs[:2]
    u_refs = refs[2:2 + n_groups]
    hg_ref, cw_ref, cb_ref, lng_ref, lnb_ref, wa_ref, wb_ref, o_ref, pad_c, pad_l, cv = refs[2 + n_groups:]
    bm = o_ref.shape[0]
    m = pl.program_id(0)
    taps = cw_ref.shape[0]
    half = taps // 2

    def u_tile(col):
        tile = col // LANES
        return u_refs[tile // heads][tile % heads].astype(F32)

    def u_cols(col, width):
        return jnp.concatenate([u_tile(col + i * LANES) for i in range(width // LANES)], axis=1)

    o_h = [of_ref[h].astype(F32) + ob_ref[h].astype(F32) for h in range(heads)]
    ss = o_h[0] * o_h[0]
    for h in range(1, heads):
        ss = ss + o_h[h] * o_h[h]
    inv = lax.rsqrt(jnp.sum(ss, axis=-1, keepdims=True) / d_a + EPS)
    gated = []
    for h in range(heads):
        sl = slice(h * LANES, (h + 1) * LANES)
        gated.append((o_h[h] * inv * hg_ref[:, sl]) * _silu(u_tile(h * LANES)))
    y_a = jnp.dot(jnp.concatenate(gated, axis=1).astype(BF16), wa_ref[...], preferred_element_type=F32)

    glu = u_cols(d_a, d_b) * jax.nn.sigmoid(u_cols(d_a + d_b, d_b))

    n_tiles = d_b // LANES

    def conv(pad, seg):
        stride = seg + 2 * CONV_PAD
        zeros = jnp.zeros((CONV_PAD, n_tiles, LANES), F32)
        for i in range(bm // seg):
            pad[i * stride:i * stride + CONV_PAD] = zeros
            pad[i * stride + CONV_PAD + seg:(i + 1) * stride] = zeros
            for s in range(n_tiles):
                pad[i * stride + CONV_PAD:i * stride + CONV_PAD + seg, s, :] = (
                    glu[i * seg:(i + 1) * seg, s * LANES:(s + 1) * LANES])
        rows = SCAN_CHUNK

        def time_step(ci, carry):
            t0 = ci * rows
            p0 = t0 + (t0 // seg) * 2 * CONV_PAD + CONV_PAD - half
            acc = jnp.zeros((rows, n_tiles, LANES), F32)
            for j in range(taps):
                acc = acc + pad[pl.ds(p0 + j, rows)] * cw_ref[j]
            cv[pl.ds(t0, rows)] = acc
            return carry

        lax.fori_loop(0, bm // rows, time_step, 0)

    @pl.when(m < n_ctx_blocks)
    def _():
        conv(pad_c, seg_ctx)

    @pl.when(m >= n_ctx_blocks)
    def _():
        conv(pad_l, seg_lat)

    v = jnp.concatenate([cv[:, s, :] for s in range(n_tiles)], axis=1) + cb_ref[...]
    mu = jnp.mean(v, axis=-1, keepdims=True)
    var = jnp.mean(jnp.square(v - mu), axis=-1, keepdims=True)
    v = (v - mu) * lax.rsqrt(var + EPS) * lng_ref[...] + lnb_ref[...]
    y_b = jnp.dot(_silu(v).astype(BF16), wb_ref[...], preferred_element_type=F32)

    g0 = d_a + 2 * d_b
    g_a = jax.nn.sigmoid(u_cols(g0, d_model))
    g_b = jax.nn.sigmoid(u_cols(g0 + d_model, d_model))
    o_ref[...] = (g_a * y_a + g_b * y_b).astype(o_ref.dtype)


def _branches(o_f, o_b, u, hg, cw, cb, lng, lnb, wa, wb, lay):
    heads, d_a, d_b, d = lay["heads"], lay["d_a"], lay["d_b"], lay["d"]
    bm = lay["bm3"]
    ncb = lay["tc"] // bm
    seg_c = min(lay["ctx_len"], bm)
    seg_l = GRID_W
    taps = cw.shape[0]
    const = lambda m: (0, 0)
    resident = dict(pipeline_mode=pl.Buffered(1))
    assert d_b % d_a == 0 and d % d_a == 0
    n_groups = (d_a + 2 * d_b + 2 * d) // d_a

    def pad_rows(seg):
        return (bm // seg) * (seg + 2 * CONV_PAD)

    def group(i):
        return pl.BlockSpec((heads, bm, LANES), lambda m: (4 + i, m, 0))

    return pl.pallas_call(
        functools.partial(_branches_kernel, n_groups=n_groups, n_ctx_blocks=ncb, seg_ctx=seg_c, seg_lat=seg_l,
                          heads=heads, d_a=d_a, d_b=d_b, d_model=d),
        grid=(lay["t"] // bm,),
        in_specs=[
            pl.BlockSpec((heads, bm, LANES), lambda m: (0, m, 0)),
            pl.BlockSpec((heads, bm, LANES), lambda m: (0, m, 0)),
            *[group(i) for i in range(n_groups)],
            pl.BlockSpec((1, d_a), const),
            pl.BlockSpec((taps, d_b // LANES, LANES), lambda m: (0, 0, 0)),
            pl.BlockSpec((1, d_b), const),
            pl.BlockSpec((1, d_b), const),
            pl.BlockSpec((1, d_b), const),
            pl.BlockSpec((d_a, d), const, **resident),
            pl.BlockSpec((d_b, d), const, **resident),
        ],
        out_specs=pl.BlockSpec((bm, d), lambda m: (m, 0)),
        out_shape=jax.ShapeDtypeStruct((lay["t"], d), BF16),
        scratch_shapes=[
            pltpu.VMEM((pad_rows(seg_c), d_b // LANES, LANES), F32),
            pltpu.VMEM((pad_rows(seg_l), d_b // LANES, LANES), F32),
            pltpu.VMEM((bm, d_b // LANES, LANES), F32),
        ],
        compiler_params=_params("parallel"),
        name="branches_merge",
    )(o_f, o_b, *([u] * n_groups), hg, cw, cb, lng, lnb, wa, wb)


def _pack_bf16_pairs(lo, hi):
    lo_bits = lax.bitcast_convert_type(lo, U32)
    hi_bits = lax.bitcast_convert_type(hi, U32)
    return (lo_bits >> 16) | (hi_bits & jnp.uint32(0xFFFF0000))


def _unpack_bf16_pairs(packed):
    lo = lax.bitcast_convert_type(packed << 16, F32)
    hi = lax.bitcast_convert_type(packed & jnp.uint32(0xFFFF0000), F32)
    return lo, hi


def _outproj_kernel(mix_ref, xp_ref, xs_ref, g1_ref, ng_ref, sh2_ref, sc2_ref, wo_ref, wr_ref, br_ref,
                    x1_ref, hp_ref, rt_ref, cnt_ref, *, n_ctx_blocks, n_experts):
    m = pl.program_id(0)

    @pl.when(m == 0)
    def _():
        cnt_ref[...] = jnp.zeros(cnt_ref.shape, F32)

    x = jnp.where(m < n_ctx_blocks, xp_ref[...], xs_ref[...])
    y = jnp.dot(mix_ref[...], wo_ref[...], preferred_element_type=F32)
    x1 = x + g1_ref[...] * y
    x1_ref[...] = x1
    hb = _mod_rmsnorm(x1, ng_ref[...], sc2_ref[...], sh2_ref[...]).astype(BF16)
    hf = hb.astype(F32)
    half = hf.shape[1] // 2
    hp_ref[...] = _pack_bf16_pairs(hf[:, :half], hf[:, half:])

    logits = jnp.dot(hb, wr_ref[...], preferred_element_type=F32) + br_ref[...]
    lane = lax.broadcasted_iota(I32, logits.shape, 1).astype(F32)
    logits = jnp.where(lane < n_experts, logits, -jnp.inf)
    vals, idxs = [], []
    for _ in range(TOP_K):
        mx = jnp.max(logits, axis=-1, keepdims=True)
        ix = jnp.min(jnp.where(logits == mx, lane, float(LANES)), axis=-1, keepdims=True)
        vals.append(mx)
        idxs.append(ix)
        logits = jnp.where(lane == ix, -jnp.inf, logits)
    es = [jnp.exp(v - vals[0]) for v in vals]
    tot = es[0]
    for e in es[1:]:
        tot = tot + e
    hot = [(lane == ix).astype(F32) for ix in idxs]
    hot_all = hot[0] + hot[1] + hot[2] + hot[3]
    bm = hot_all.shape[0]
    earlier = (lax.broadcasted_iota(I32, (bm, bm), 1) < lax.broadcasted_iota(I32, (bm, bm), 0)).astype(BF16)
    before = jnp.dot(earlier, hot_all.astype(BF16), preferred_element_type=F32) + cnt_ref[...]
    out = jnp.zeros(logits.shape, F32)
    for k in range(TOP_K):
        out = jnp.where(lane == k, idxs[k], out)
        out = jnp.where(lane == TOP_K + k, es[k] / tot, out)
        out = jnp.where(lane == 2 * TOP_K + k, jnp.sum(before * hot[k], axis=-1, keepdims=True), out)
    rt_ref[...] = out
    cnt_ref[...] = before[bm - 1:bm, :] + hot_all[bm - 1:bm, :]


def _outproj(mix, xp, xs, mods, ng, wo, wr, br, lay):
    d = lay["d"]
    bm = lay["bm3"]
    ncb = lay["tc"] // bm
    t = lay["t"]

    def seq_row(m):
        return jnp.where(m < ncb, lay["ctx_row"], ((m - ncb) * bm) // lay["lat_len"])

    const = lambda m: (0, 0)
    mod = lambda part: pl.BlockSpec((None, 1, d), lambda m: (seq_row(m) * 6 + part, 0, 0))
    return pl.pallas_call(
        functools.partial(_outproj_kernel, n_ctx_blocks=ncb, n_experts=lay["experts"]),
        grid=(t // bm,),
        in_specs=[
            pl.BlockSpec((bm, d), lambda m: (m, 0)),
            pl.BlockSpec((bm, d), lambda m: (jnp.minimum(m, ncb - 1), 0)),
            pl.BlockSpec((bm, d), lambda m: (jnp.maximum(m - ncb, 0), 0)),
            mod(2),
            pl.BlockSpec((1, d), const),
            mod(3),
            mod(4),
            pl.BlockSpec((d, d), const, pipeline_mode=pl.Buffered(1)),
            pl.BlockSpec((d, LANES), const),
            pl.BlockSpec((1, LANES), const),
        ],
        out_specs=[
            pl.BlockSpec((bm, d), lambda m: (m, 0)),
            pl.BlockSpec((bm, d // 2), lambda m: (m, 0)),
            pl.BlockSpec((bm, LANES), lambda m: (m, 0)),
            pl.BlockSpec((1, LANES), const),
        ],
        out_shape=[
            jax.ShapeDtypeStruct((t, d), F32),
            jax.ShapeDtypeStruct((t, d // 2), U32),
            jax.ShapeDtypeStruct((t, LANES), F32),
            jax.ShapeDtypeStruct((1, LANES), F32),
        ],
        compiler_params=_params("arbitrary"),
        name="outproj_router",
    )(mix, xp, xs, mods, ng, mods, mods, wo, wr, br)


def _routing_tables(rt, counts_row, lay):
    e, bm = lay["experts"], lay["bm_moe"]
    nb = lay["moe_blocks"]
    idx = rt[:, :TOP_K].astype(I32)
    rank = rt[:, 2 * TOP_K:3 * TOP_K].astype(I32)
    counts = counts_row[0, :e].astype(I32)
    nblk = (counts + bm - 1) // bm
    blk_end = jnp.cumsum(nblk).astype(I32)
    blk_start = blk_end - nblk
    experts = jnp.arange(e, dtype=I32)
    first_slot = jnp.sum(jnp.where(idx[:, :, None] == experts, blk_start * bm, 0), axis=-1)
    pos = first_slot + rank
    n_real = blk_end[-1]

    def lookup(table, i):
        return jnp.sum(jnp.where(i[:, None] == experts, table, 0), axis=-1)

    def schedule(nt):
        s = jnp.arange(nb * nt, dtype=I32)
        ex = jnp.minimum(jnp.sum((blk_end * nt <= s[:, None]).astype(I32), axis=-1), e - 1)
        r = s - lookup(blk_start, ex) * nt
        per = jnp.maximum(lookup(nblk, ex), 1)
        real = s < n_real * nt
        tail = s - n_real * nt
        m_idx = jnp.where(real, lookup(blk_start, ex) + r % per, n_real + tail // nt)
        n_out = jnp.where(real, r // per, tail % nt)
        is_last = s == jnp.maximum(n_real * nt - 1, 0)
        e_last = jnp.sum(jnp.where(is_last, ex, 0))
        n_last = jnp.sum(jnp.where(is_last, r // per, 0))
        e_w = jnp.where(real, ex, e_last)
        n_w = jnp.where(real, n_out, n_last)
        flag = jnp.where(real, 1 + 2 * (r % per == 0).astype(I32), 0)
        return m_idx.astype(I32), n_out.astype(I32), e_w.astype(I32), n_w.astype(I32), flag.astype(I32)

    return pos, schedule(lay["nt_up"]), schedule(lay["nt_down"])


def _dispatch_kernel(pos_ref, h_ref, zero_ref, out_ref, sem):
    del zero_ref
    bt = h_ref.shape[0] * SUBLANES

    def issue(i, carry):
        for u in range(SUBLANES):
            for k in range(TOP_K):
                slot = pos_ref[0, k * bt + i * SUBLANES + u]
                pltpu.make_async_copy(h_ref.at[i, pl.ds(u, 1), :], out_ref.at[pl.ds(slot, 1), :], sem).start(
                    priority=k % 2)
        return carry

    lax.fori_loop(0, bt // SUBLANES, issue, 0)
    n = TOP_K * bt
    pltpu.make_async_copy(out_ref.at[pl.ds(0, n), :], out_ref.at[pl.ds(0, n), :], sem).wait()


def _dispatch_rows(hp, pos_blocks, lay):
    bt = lay["bt"]
    t, width = hp.shape
    rows = lay["moe_blocks"] * lay["bm_moe"]
    return pl.pallas_call(
        _dispatch_kernel,
        grid=(t // bt,),
        in_specs=[
            pl.BlockSpec((None, 1, TOP_K * bt), lambda m: (m, 0, 0), memory_space=pltpu.SMEM),
            pl.BlockSpec((bt // SUBLANES, SUBLANES, width), lambda m: (m, 0, 0)),
            pl.BlockSpec(memory_space=pl.ANY),
        ],
        out_specs=pl.BlockSpec(memory_space=pl.ANY),
        out_shape=jax.ShapeDtypeStruct((rows, width), hp.dtype),
        scratch_shapes=[pltpu.SemaphoreType.DMA(())],
        input_output_aliases={2: 0},
        compiler_params=_params("arbitrary"),
        name="moe_dispatch",
    )(pos_blocks, hp.reshape(t // SUBLANES, SUBLANES, width), jnp.zeros((rows, width), hp.dtype))


def _up_kernel(sm, sn, se, sw, sf, x_ref, wg_ref, wu_ref, bg_ref, bu_ref, o_ref, wg_bf, wu_bf):
    flag = sf[pl.program_id(0)]

    @pl.when(flag >= 2)
    def _():
        wg_bf[...] = wg_ref[...].astype(BF16)
        wu_bf[...] = wu_ref[...].astype(BF16)

    @pl.when(flag >= 1)
    def _():
        lo, hi = _unpack_bf16_pairs(x_ref[...])
        x = jnp.concatenate([lo.astype(BF16), hi.astype(BF16)], axis=1)
        gate = jnp.dot(x, wg_bf[...], preferred_element_type=F32) + bg_ref[...]
        up = jnp.dot(x, wu_bf[...], preferred_element_type=F32) + bu_ref[...]
        gate = jnp.minimum(gate, SWIGLU_LIMIT)
        up = jnp.clip(up, -SWIGLU_LIMIT, SWIGLU_LIMIT)
        o_ref[...] = ((up + 1.0) * gate * jax.nn.sigmoid(SWIGLU_ALPHA * gate)).astype(o_ref.dtype)

    @pl.when(flag == 0)
    def _():
        o_ref[...] = jnp.zeros(o_ref.shape, o_ref.dtype)


def _moe_up(xs, w_gu, b_gu, sched, lay):
    bm, nb, nt = lay["bm_moe"], lay["moe_blocks"], lay["nt_up"]
    d, ff = lay["d"], lay["d_ff"]
    tn = ff // nt
    e = lay["experts"]
    grid_spec = pltpu.PrefetchScalarGridSpec(
        num_scalar_prefetch=5,
        grid=(nb * nt,),
        in_specs=[
            pl.BlockSpec((bm, d // 2), lambda s, sm, sn, se, sw, sf: (sm[s], 0)),
            pl.BlockSpec((None, d, tn), lambda s, sm, sn, se, sw, sf: (se[s], 0, sw[s])),
            pl.BlockSpec((None, d, tn), lambda s, sm, sn, se, sw, sf: (se[s], 0, nt + sw[s])),
            pl.BlockSpec((None, 1, tn), lambda s, sm, sn, se, sw, sf: (se[s], 0, sw[s])),
            pl.BlockSpec((None, 1, tn), lambda s, sm, sn, se, sw, sf: (se[s], 0, nt + sw[s])),
        ],
        out_specs=pl.BlockSpec((bm, tn), lambda s, sm, sn, se, sw, sf: (sm[s], sn[s])),
        scratch_shapes=[pltpu.VMEM((d, tn), BF16), pltpu.VMEM((d, tn), BF16)],
    )
    return pl.pallas_call(
        _up_kernel,
        grid_spec=grid_spec,
        out_shape=jax.ShapeDtypeStruct((nb * bm, ff), BF16),
        compiler_params=_params("arbitrary"),
        name="moe_gate_up",
    )(*sched, xs, w_gu, w_gu, b_gu.reshape(e, 1, 2 * ff), b_gu.reshape(e, 1, 2 * ff))


def _down_kernel(sm, sn, se, sw, sf, a_ref, wl_ref, wh_ref, bl_ref, bh_ref, o_ref, wl_bf, wh_bf):
    flag = sf[pl.program_id(0)]

    @pl.when(flag >= 2)
    def _():
        wl_bf[...] = wl_ref[...].astype(BF16)
        wh_bf[...] = wh_ref[...].astype(BF16)

    @pl.when(flag >= 1)
    def _():
        a = a_ref[...]
        lo = jnp.dot(a, wl_bf[...], preferred_element_type=F32) + bl_ref[...]
        hi = jnp.dot(a, wh_bf[...], preferred_element_type=F32) + bh_ref[...]
        o_ref[...] = _pack_bf16_pairs(lo.astype(BF16).astype(F32), hi.astype(BF16).astype(F32))

    @pl.when(flag == 0)
    def _():
        o_ref[...] = jnp.zeros(o_ref.shape, o_ref.dtype)


def _moe_down(act, w_dn, b_dn, sched, lay):
    bm, nb, nt = lay["bm_moe"], lay["moe_blocks"], lay["nt_down"]
    d, ff = lay["d"], lay["d_ff"]
    half = d // 2
    tn = half // nt
    e = lay["experts"]
    grid_spec = pltpu.PrefetchScalarGridSpec(
        num_scalar_prefetch=5,
        grid=(nb * nt,),
        in_specs=[
            pl.BlockSpec((bm, ff), lambda s, sm, sn, se, sw, sf: (sm[s], 0)),
            pl.BlockSpec((None, ff, tn), lambda s, sm, sn, se, sw, sf: (se[s], 0, sw[s])),
            pl.BlockSpec((None, ff, tn), lambda s, sm, sn, se, sw, sf: (se[s], 0, nt + sw[s])),
            pl.BlockSpec((None, 1, tn), lambda s, sm, sn, se, sw, sf: (se[s], 0, sw[s])),
            pl.BlockSpec((None, 1, tn), lambda s, sm, sn, se, sw, sf: (se[s], 0, nt + sw[s])),
        ],
        out_specs=pl.BlockSpec((bm, tn), lambda s, sm, sn, se, sw, sf: (sm[s], sn[s])),
        scratch_shapes=[pltpu.VMEM((ff, tn), BF16), pltpu.VMEM((ff, tn), BF16)],
    )
    return pl.pallas_call(
        _down_kernel,
        grid_spec=grid_spec,
        out_shape=jax.ShapeDtypeStruct((nb * bm, half), U32),
        compiler_params=_params("arbitrary"),
        name="moe_down",
    )(*sched, act, w_dn, w_dn, b_dn.reshape(e, 1, d), b_dn.reshape(e, 1, d))


def _combine_kernel(pos_ref, y_ref, rt_ref, x1_ref, g2_ref, fg_ref, o_ref, buf, sem):
    bt = o_ref.shape[0]
    n = pos_ref.shape[1]

    def issue(i, carry):
        for u in range(SUBLANES):
            row = pos_ref[0, i * SUBLANES + u]
            pltpu.make_async_copy(y_ref.at[pl.ds(row, 1), :], buf.at[i, pl.ds(u, 1), :], sem).start(
                priority=u % 2)
        return carry

    lax.fori_loop(0, n // SUBLANES, issue, 0)
    pltpu.make_async_copy(buf, buf, sem).wait()
    lo = hi = None
    g = bt // SUBLANES
    for k in range(TOP_K):
        p = rt_ref[:, TOP_K + k:TOP_K + k + 1]
        lo_k, hi_k = _unpack_bf16_pairs(buf[k * g:(k + 1) * g].reshape(bt, buf.shape[2]))
        lo = lo_k * p if lo is None else lo + lo_k * p
        hi = hi_k * p if hi is None else hi + hi_k * p
    x2 = x1_ref[...] + g2_ref[...] * jnp.concatenate([lo, hi], axis=1)
    o_ref[...] = x2 * lax.rsqrt(jnp.mean(x2 * x2, axis=-1, keepdims=True) + EPS) * fg_ref[...]


def _combine(y_sorted, pos, rt, x1, mods, fg, lay, *, blk0, n_tok, seq_len, row0):
    d = lay["d"]
    bt = lay["bt"]
    nblk = n_tok // bt

    def seq_row(m):
        return row0 if seq_len is None else row0 + (m * bt) // seq_len

    return pl.pallas_call(
        _combine_kernel,
        grid=(nblk,),
        in_specs=[
            pl.BlockSpec((None, 1, TOP_K * bt), lambda m: (blk0 + m, 0, 0), memory_space=pltpu.SMEM),
            pl.BlockSpec(memory_space=pl.ANY),
            pl.BlockSpec((bt, LANES), lambda m: (blk0 + m, 0)),
            pl.BlockSpec((bt, d), lambda m: (blk0 + m, 0)),
            pl.BlockSpec((None, 1, d), lambda m: (seq_row(m) * 6 + 5, 0, 0)),
            pl.BlockSpec((1, d), lambda m: (0, 0)),
        ],
        out_specs=pl.BlockSpec((bt, d), lambda m: (m, 0)),
        out_shape=jax.ShapeDtypeStruct((n_tok, d), F32),
        scratch_shapes=[pltpu.VMEM((TOP_K * bt // SUBLANES, SUBLANES, d // 2), U32), pltpu.SemaphoreType.DMA(())],
        compiler_params=_params("arbitrary"),
        name="moe_combine_final",
    )(pos, y_sorted, rt, x1, mods, fg)


def _layout(x_prompt, x_sample, state_hgrn, w_gate_up):
    cb, cl, d = x_prompt.shape
    lb_, ll, _ = x_sample.shape
    heads = state_hgrn.shape[3]
    experts, _, ff2 = w_gate_up.shape[1:]
    tc, tl = cb * cl, lb_ * ll
    t = tc + tl
    bm_moe = 512 if t * TOP_K >= 32768 else 128
    lay = dict(
        d=d, t=t, tc=tc, tl=tl, ctx_batch=cb, ctx_len=cl, lat_batch=lb_, lat_len=ll, ctx_row=lb_,
        heads=heads, d_a=heads * LANES, experts=experts, d_ff=ff2 // 2,
        bm1=512 if (tc % 512 == 0 and ll % 512 == 0) else cl, bn1=min(1024, heads * LANES), lc=cl, bm3=cl,
        bt=min(256, cl),
        bm_moe=bm_moe, moe_blocks=pl.cdiv(t * TOP_K, bm_moe) + experts,
        nt_up=max(1, (ff2 // 2) // 1024), nt_down=max(1, (d // 2) // 1024),
    )
    assert state_hgrn.shape[4] == LANES and state_hgrn.shape[5] == LANES
    assert tc % lay["bm1"] == 0 and tl % lay["bm1"] == 0 and ll % lay["bm1"] == 0
    assert ll % cl == 0 and cl % GRID_W == 0 and cl % SCAN_CHUNK == 0
    return lay


def kernel(x_prompt, x_sample, state_hgrn, c, c_ctx, w_ada, b_ada, norm_mix_g, w_in, lb_logits, hgrn_norm_g, w_a_out, conv_w, conv_b, conv_ln_g, conv_ln_b, w_b_out, w_out, norm_moe_g, w_router, b_router, w_gate_up, b_gate_up, w_down, b_down, final_norm_g):
    assert w_ada.shape[0] == 1, "one trunk layer"
    lay = _layout(x_prompt, x_sample, state_hgrn, w_gate_up)
    d, d_a, heads = lay["d"], lay["d_a"], lay["heads"]
    d_b = conv_w.shape[2]
    lay["d_b"] = d_b
    xp = x_prompt.reshape(lay["tc"], d)
    xs = x_sample.reshape(lay["tl"], d)

    n_rows = lay["lat_batch"] + 1
    rows_pad = pl.cdiv(n_rows, SUBLANES) * SUBLANES
    cond = jnp.concatenate([c, c_ctx[None, :], jnp.zeros((rows_pad - n_rows, d), F32)], axis=0)
    mods = _adaln(cond, w_ada.reshape(d, 6 * d), b_ada.reshape(1, 6 * d)).reshape(rows_pad * 6, 1, d)

    w_in_bf = w_in.reshape(d, -1).astype(BF16)
    u = _inproj(xp, xs, norm_mix_g.reshape(1, d), mods, w_in_bf, lay)

    lb_all = jnp.cumsum(jax.nn.softmax(lb_logits.astype(F32), axis=0), axis=0)[0]
    s0 = state_hgrn.reshape(lay["lat_batch"], 2, heads, LANES, LANES)
    o_f, o_b, s_fin = _hgrn(u, lb_all.reshape(2, heads, 1, LANES), s0, lay)

    mix = _branches(o_f, o_b, u, hgrn_norm_g.reshape(1, d_a), conv_w.reshape(-1, d_b // LANES, LANES), conv_b.reshape(1, d_b),
                    conv_ln_g.reshape(1, d_b), conv_ln_b.reshape(1, d_b), w_a_out.reshape(d_a, d).astype(BF16),
                    w_b_out.reshape(d_b, d).astype(BF16), lay)

    e, ff = lay["experts"], lay["d_ff"]
    wr = jnp.zeros((d, LANES), BF16).at[:, :e].set(w_router.reshape(d, e).astype(BF16))
    br = jnp.zeros((1, LANES), F32).at[:, :e].set(b_router.reshape(1, e))
    x1, hp, rt, counts = _outproj(mix, xp, xs, mods, norm_moe_g.reshape(1, d), w_out.reshape(d, d).astype(BF16), wr,
                                  br, lay)

    pos, sched_up, sched_down = _routing_tables(rt, counts, lay)
    bt = lay["bt"]
    pos_blocks = pos.reshape(lay["t"] // bt, bt, TOP_K).transpose(0, 2, 1).reshape(lay["t"] // bt, 1, TOP_K * bt)
    x_sorted = _dispatch_rows(hp, pos_blocks, lay)
    act = _moe_up(x_sorted, w_gate_up.reshape(e, d, 2 * ff), b_gate_up.reshape(e, 2 * ff), sched_up, lay)
    y_sorted = _moe_down(act, w_down.reshape(e, ff, d), b_down.reshape(e, d), sched_down, lay)

    fg = final_norm_g.reshape(1, d)
    y_prompt = _combine(y_sorted, pos_blocks, rt, x1, mods, fg, lay, blk0=0, n_tok=lay["tc"],
                        seq_len=None, row0=lay["ctx_row"])
    y_sample = _combine(y_sorted, pos_blocks, rt, x1, mods, fg, lay, blk0=lay["tc"] // bt, n_tok=lay["tl"],
                        seq_len=lay["lat_len"], row0=0)
    return (y_prompt.reshape(x_prompt.shape), y_sample.reshape(x_sample.shape),
            s_fin.reshape(lay["ctx_batch"], 1, 2, heads, LANES, LANES))
```
